```python
import jax, jax.numpy as jnp
from jax import lax
import numpy as np

D_MODEL = 1024
BATCH = 2
SEQ = 16384
DEPTH = 4

D_MIX = D_MODEL
CONV_W = D_MIX // 4
CONV_K = 31
POOL_W = D_MIX // 4
POOL_WINDOWS = (2, 4, 8, 16)
POOL_G = 4
POOL_GW = POOL_W // POOL_G
HEAD_DIM = 64
ATTN_W = D_MIX // 2
N_HEADS = ATTN_W // HEAD_DIM
IDX_HEADS = 4
IDX_DIM = 64
TOPK_MAX = 256
Q_BLOCK = 128
ROPE_THETA = 10000.0
PLE_DIM = 256
EPS = 1e-6
SPLITS = (CONV_W, CONV_W, CONV_W,
          POOL_W, POOL_W,
          ATTN_W, ATTN_W, ATTN_W, ATTN_W,
          IDX_HEADS * IDX_DIM, IDX_DIM, IDX_HEADS)
D_IN = 3 * CONV_W + 2 * POOL_W + 4 * ATTN_W + IDX_HEADS * IDX_DIM + IDX_DIM + IDX_HEADS

kernel_name = "hybrid_conv_pool_dsa_trunk"


def rmsnorm(x, g):
    xf = x.astype(jnp.float32)
    y = xf * lax.rsqrt(jnp.mean(xf * xf, axis=-1, keepdims=True) + EPS)
    return (y * g.astype(jnp.float32)).astype(x.dtype)


def layernorm(x, g, b):
    xf = x.astype(jnp.float32)
    mu = jnp.mean(xf, axis=-1, keepdims=True)
    xc = xf - mu
    y = xc * lax.rsqrt(jnp.mean(xc * xc, axis=-1, keepdims=True) + EPS)
    return (y * g.astype(jnp.float32) + b.astype(jnp.float32)).astype(x.dtype)


def rope(x, pos):
    half = x.shape[-1] // 2
    inv = ROPE_THETA ** (-jnp.arange(half, dtype=jnp.float32) / half)
    ang = pos.astype(jnp.float32)[:, None] * inv[None, :]
    cos = jnp.cos(ang)[None, :, None, :]
    sin = jnp.sin(ang)[None, :, None, :]
    xf = x.astype(jnp.float32)
    x1, x2 = xf[..., :half], xf[..., half:]
    return jnp.concatenate([x1 * cos - x2 * sin, x2 * cos + x1 * sin], axis=-1).astype(x.dtype)


def conv_module(val, glu, dw_w, dw_b, ln_g, ln_b, pw_w, pw_b):
    u = val * jax.nn.sigmoid(glu)
    u = lax.conv_general_dilated(u, dw_w[:, None, :], window_strides=(1,),
                                 padding=[(CONV_K - 1, 0)],
                                 dimension_numbers=('NWC', 'WIO', 'NWC'),
                                 feature_group_count=CONV_W) + dw_b
    u = jax.nn.silu(layernorm(u, ln_g, ln_b))
    return u @ pw_w + pw_b


def pool_mixer(u, w, b, scale):
    B, S, _ = u.shape
    ug = u.reshape(B, S, POOL_G, POOL_GW)
    t = jnp.arange(S)
    outs = []
    for g, win in enumerate(POOL_WINDOWS):
        xg = ug[:, :, g].astype(jnp.float32)
        cs = jnp.cumsum(xg, axis=1)
        lag = jnp.pad(cs, ((0, 0), (win, 0), (0, 0)))[:, :S]
        cnt = jnp.minimum(t + 1, win).astype(jnp.float32)[None, :, None]
        outs.append((cs - lag) / cnt - xg)
    d = jnp.stack(outs, axis=2).astype(u.dtype)
    y = jnp.einsum('bsgc,gcd->bsgd', d, w).reshape(B, S, POOL_W) + b
    return y * scale


def sparse_attention(q, k, v, qi, ki, wi):
    B, S, H, Dh = q.shape
    L = k.shape[1]
    topk = min(TOPK_MAX, L // 4)
    nb = S // Q_BLOCK
    key_pos = jnp.arange(L)
    kf = ki.astype(jnp.float32)

    def to_blocks(a):
        return a.reshape((B, nb, Q_BLOCK) + a.shape[2:]).swapaxes(0, 1)

    def gather(a, i):
        return a[i]

    def block(args):
        qb, qib, wib, start = args
        qpos = start + jnp.arange(Q_BLOCK)
        causal = key_pos[None, :] <= qpos[:, None]
        s = jnp.einsum('bqhd,bsd->bqhs', qib.astype(jnp.float32), kf) * (IDX_DIM ** -0.5)
        score = jnp.einsum('bqh,bqhs->bqs', wib.astype(jnp.float32), jax.nn.relu(s))
        score = jnp.where(causal[None], score, -jnp.inf)
        _, idx = lax.top_k(score, topk)
        kg = jax.vmap(gather)(k, idx)
        vg = jax.vmap(gather)(v, idx)
        logits = jnp.einsum('bqhd,bqkhd->bhqk', qb.astype(jnp.float32),
                            kg.astype(jnp.float32)) * (HEAD_DIM ** -0.5)
        valid = idx <= qpos[None, :, None]
        logits = jnp.where(valid[:, None], logits, -jnp.inf)
        pr = jax.nn.softmax(logits, axis=-1)
        o = jnp.einsum('bhqk,bqkhd->bqhd', pr, vg.astype(jnp.float32))
        return o.astype(q.dtype)

    starts = jnp.arange(nb) * Q_BLOCK
    o = lax.map(block, (to_blocks(q), to_blocks(qi), to_blocks(wi), starts))
    return o.swapaxes(0, 1).reshape(B, S, H * Dh)


def setup_inputs(seed: int = 0) -> dict:
    key = jax.random.key(seed)
    ks = jax.random.split(key, 20)
    f32 = jnp.float32

    def nrm(k, shape, scale):
        return jax.random.normal(k, shape, f32) * scale

    return {
        'x': nrm(ks[0], (BATCH, SEQ, D_MODEL), 1.0),
        'p': nrm(ks[1], (DEPTH, BATCH, SEQ, PLE_DIM), 1.0),
        'norm_g': 1.0 + nrm(ks[2], (DEPTH, D_MODEL), 0.1),
        'w_in': nrm(ks[3], (DEPTH, D_MODEL, D_IN), D_MODEL ** -0.5),
        'b_in': nrm(ks[4], (DEPTH, D_IN), 0.02),
        'conv_dw_w': nrm(ks[5], (DEPTH, CONV_K, CONV_W), CONV_K ** -0.5),
        'conv_dw_b': nrm(ks[6], (DEPTH, CONV_W), 0.02),
        'conv_ln_g': 1.0 + nrm(ks[7], (DEPTH, CONV_W), 0.1),
        'conv_ln_b': nrm(ks[8], (DEPTH, CONV_W), 0.02),
        'conv_pw_w': nrm(ks[9], (DEPTH, CONV_W, CONV_W), CONV_W ** -0.5),
        'conv_pw_b': nrm(ks[10], (DEPTH, CONV_W), 0.02),
        'pool_w': nrm(ks[11], (DEPTH, POOL_G, POOL_GW, POOL_GW), POOL_GW ** -0.5),
        'pool_b': nrm(ks[12], (DEPTH, POOL_W), 0.02),
        'pool_scale': 1.0 + nrm(ks[13], (DEPTH, POOL_W), 0.1),
        'w_out': nrm(ks[14], (DEPTH, D_MIX, D_MODEL), D_MIX ** -0.5),
        'ple_w': nrm(ks[15], (DEPTH, PLE_DIM, D_MODEL), PLE_DIM ** -0.5),
        'ple_gate_w': nrm(ks[16], (DEPTH, D_MODEL, D_MODEL), D_MODEL ** -0.5),
        'final_norm_g': 1.0 + nrm(ks[17], (D_MODEL,), 0.1),
    }


def reference(x, p, norm_g, w_in, b_in, conv_dw_w, conv_dw_b, conv_ln_g, conv_ln_b,
              conv_pw_w, conv_pw_b, pool_w, pool_b, pool_scale, w_out, ple_w,
              ple_gate_w, final_norm_g):
    B, S, _ = x.shape
    pos = jnp.arange(S)
    cut = [int(c) for c in np.cumsum(SPLITS)[:-1]]
    h = x
    for i in range(DEPTH):
        n = rmsnorm(h, norm_g[i])
        z = n @ w_in[i] + b_in[i]
        (c_val, c_glu, c_gate, p_in, p_gate, q, k, v, a_gate,
         qi, ki, wi) = jnp.split(z, cut, axis=-1)
        ya = conv_module(c_val, c_glu, conv_dw_w[i], conv_dw_b[i], conv_ln_g[i],
                         conv_ln_b[i], conv_pw_w[i], conv_pw_b[i]) * jax.nn.silu(c_gate)
        yb = pool_mixer(p_in, pool_w[i], pool_b[i], pool_scale[i]) * jax.nn.silu(p_gate)
        q = rope(q.reshape(B, S, N_HEADS, HEAD_DIM), pos)
        k = rope(k.reshape(B, S, N_HEADS, HEAD_DIM), pos)
        v = v.reshape(B, S, N_HEADS, HEAD_DIM)
        qi = rope(qi.reshape(B, S, IDX_HEADS, IDX_DIM), pos)
        ki = rope(ki[:, :, None, :], pos)[:, :, 0, :]
        yc = sparse_attention(q, k, v, qi, ki, wi) * jax.nn.silu(a_gate)
        h = h + jnp.concatenate([ya, yb, yc], axis=-1) @ w_out[i]
        h = h + (p[i] @ ple_w[i]) * jax.nn.sigmoid(h @ ple_gate_w[i])
    return rmsnorm(h, final_norm_g)
```

```python
import functools

import jax
import jax.numpy as jnp
import numpy as np
from jax import lax
from jax.experimental import pallas as pl
from jax.experimental.pallas import tpu as pltpu

F32 = jnp.float32
BF16 = jnp.bfloat16

CONV_W = 256
CONV_K = 31
POOL_W = 256
POOL_GW = 64
POOL_WINDOWS = (2, 4, 8, 16)
HEAD_DIM = 64
ATTN_W = 512
N_HEADS = 8
IDX_HEADS = 4
IDX_DIM = 64
TOPK_MAX = 256
ROPE_THETA = 10000.0
EPS = 1e-6

Q_BLOCK = 128
KEY_CHUNK = 256
HALO = 32
ROW_TILE = 512
IDX_PAD = 384
N_MAIN = 3 * CONV_W + 2 * POOL_W + 4 * ATTN_W
MASK_BIAS = -1e30
NEED_ALL = 1e9
VMEM_LIMIT = 60 * 1024 * 1024
HIGHEST = lax.Precision.HIGHEST


def _sigmoid(x):
    return 1.0 / (1.0 + jnp.exp(-x))


def _silu(x):
    return x * _sigmoid(x)


def _dot(a, b, precision=None):
    return lax.dot_general(a, b, (((1,), (0,)), ((), ())), precision=precision,
                           preferred_element_type=F32)


def _rope_slab(x, cos, sin):
    lane = lax.broadcasted_iota(jnp.int32, x.shape, 1)
    swapped = jnp.where((lane & 32) == 0, pltpu.roll(x, 96, 1), pltpu.roll(x, 32, 1))
    return x * cos + swapped * sin


def _rope(x, cos, sin):
    return jnp.concatenate(
        [_rope_slab(x[:, s:s + 128], cos, sin) for s in range(0, x.shape[1], 128)], axis=1)


def _inproj_kernel(h_ref, g_ref, wm_ref, bm_ref, wx_ref, bx_ref, cos_ref, sin_ref,
                   zc_ref, zp_ref, q_ref, kt_ref, v_ref, ag_ref, qi_ref, kit_ref, ws_ref):
    x = h_ref[0]
    n = x * lax.rsqrt(jnp.mean(x * x, axis=-1, keepdims=True) + EPS) * g_ref[...]
    nb = n.astype(BF16)
    cos = cos_ref[...]
    sin = sin_ref[...]

    def proj(a, b):
        return _dot(nb, wm_ref[:, a:b]) + bm_ref[:, a:b]

    zc_ref[0] = proj(0, 768)
    zp_ref[0] = proj(768, 1280)
    q_ref[0] = (_rope(proj(1280, 1792), cos, sin) * (HEAD_DIM ** -0.5)).astype(BF16)
    kt = _rope(proj(1792, 2304), cos, sin).T.astype(BF16)
    for c in range(ROW_TILE // KEY_CHUNK):
        kt_ref[0, c] = kt[:, c * KEY_CHUNK:(c + 1) * KEY_CHUNK]
    v_ref[0] = proj(2304, 2816).astype(BF16)
    ag_ref[0] = proj(2816, 3328)

    zx = _dot(n, wx_ref[...], precision=HIGHEST) + bx_ref[...]
    qi_ref[0] = _rope(zx[:, 0:256], cos, sin) * (IDX_DIM ** -0.5)
    tail = zx[:, 256:384]
    lane = lax.broadcasted_iota(jnp.int32, tail.shape, 1)
    tail = jnp.where(lane < IDX_DIM, _rope_slab(tail, cos, sin), tail)
    ws_ref[0] = tail
    kit = tail.T[0:IDX_DIM, :]
    for c in range(ROW_TILE // KEY_CHUNK):
        kit_ref[0, c] = kit[:, c * KEY_CHUNK:(c + 1) * KEY_CHUNK]


def _inproj(h, g, wm, bm, wx, bx, cos, sin):
    B, S, D = h.shape
    nt = S // ROW_TILE
    cpt = ROW_TILE // KEY_CHUNK
    nch = S // KEY_CHUNK
    row = lambda w: pl.BlockSpec((1, ROW_TILE, w), lambda b, t: (b, t, 0))
    full = lambda a: pl.BlockSpec(a.shape, lambda b, t: (0,) * a.ndim)
    tab = pl.BlockSpec((ROW_TILE, 128), lambda b, t: (t, 0))
    out_shape = (
        jax.ShapeDtypeStruct((B, S, 768), F32),
        jax.ShapeDtypeStruct((B, S, 512), F32),
        jax.ShapeDtypeStruct((B, S, ATTN_W), BF16),
        jax.ShapeDtypeStruct((B, nch, ATTN_W, KEY_CHUNK), BF16),
        jax.ShapeDtypeStruct((B, S, ATTN_W), BF16),
        jax.ShapeDtypeStruct((B, S, ATTN_W), F32),
        jax.ShapeDtypeStruct((B, S, 256), F32),
        jax.ShapeDtypeStruct((B, nch, IDX_DIM, KEY_CHUNK), F32),
        jax.ShapeDtypeStruct((B, S, 128), F32),
    )
    out_specs = (
        row(768), row(512), row(ATTN_W),
        pl.BlockSpec((1, cpt, ATTN_W, KEY_CHUNK), lambda b, t: (b, t, 0, 0)),
        row(ATTN_W), row(ATTN_W), row(256),
        pl.BlockSpec((1, cpt, IDX_DIM, KEY_CHUNK), lambda b, t: (b, t, 0, 0)),
        row(128),
    )
    return pl.pallas_call(
        _inproj_kernel,
        grid=(B, nt),
        in_specs=[row(D), full(g), full(wm), full(bm), full(wx), full(bx), tab, tab],
        out_specs=out_specs,
        out_shape=out_shape,
        compiler_params=pltpu.CompilerParams(
            dimension_semantics=("arbitrary", "arbitrary"), vmem_limit_bytes=VMEM_LIMIT),
        name="inproj",
    )(h, g, wm, bm, wx, bx, cos, sin)


def _convpool_kernel(zc_ref, zch_ref, zp_ref, zph_ref, dw_ref, dwb_ref, lng_ref, lnb_ref,
                     pw_ref, pwb_ref, plw_ref, plb_ref, pls_ref, o_ref, ubuf, xbuf):
    t = pl.program_id(1)
    keep = (t > 0).astype(F32)

    zc = zc_ref[0]
    zch = zch_ref[0]
    ubuf[0:HALO] = zch[:, 0:256] * _sigmoid(zch[:, 256:512]) * keep
    ubuf[HALO:HALO + ROW_TILE] = zc[:, 0:256] * _sigmoid(zc[:, 256:512])
    acc = jnp.zeros((ROW_TILE, CONV_W), F32) + dwb_ref[...]
    for j in range(CONV_K):
        acc = acc + ubuf[pl.ds(HALO - (CONV_K - 1) + j, ROW_TILE), :] * dw_ref[j:j + 1, :]
    mu = jnp.mean(acc, axis=-1, keepdims=True)
    xc = acc - mu
    y = xc * lax.rsqrt(jnp.mean(xc * xc, axis=-1, keepdims=True) + EPS) * lng_ref[...] + lnb_ref[...]
    y = _silu(y)
    ya = (_dot(y.astype(BF16), pw_ref[...]) + pwb_ref[...]) * _silu(zc[:, 512:768])

    zp = zp_ref[0]
    x = zp[:, 0:256]
    xbuf[0:HALO] = zph_ref[0][:, 0:256] * keep
    xbuf[HALO:HALO + ROW_TILE] = x
    run = x
    sums = {}
    for d in range(1, POOL_WINDOWS[-1]):
        run = run + xbuf[pl.ds(HALO - d, ROW_TILE), :]
        if d + 1 in POOL_WINDOWS:
            sums[d + 1] = run
    lane = lax.broadcasted_iota(jnp.int32, (ROW_TILE, POOL_W), 1)
    grp = lane // POOL_GW
    wsum = jnp.where(grp == 0, sums[2], jnp.where(grp == 1, sums[4],
                                                  jnp.where(grp == 2, sums[8], sums[16])))
    win = jnp.where(grp == 0, 2, jnp.where(grp == 1, 4, jnp.where(grp == 2, 8, 16)))
    pos = t * ROW_TILE + lax.broadcasted_iota(jnp.int32, (ROW_TILE, POOL_W), 0)
    cnt = jnp.minimum(pos + 1, win).astype(F32)
    d = wsum / cnt - x
    yb = (_dot(d.astype(BF16), plw_ref[...]) + plb_ref[...]) * pls_ref[...] * _silu(zp[:, 256:512])

    o_ref[0] = jnp.concatenate([ya, yb], axis=1)


def _convpool(zc, zp, dw, dwb, lng, lnb, pw, pwb, plw, plb, pls):
    B, S, _ = zc.shape
    nt = S // ROW_TILE
    hpt = ROW_TILE // HALO
    row = lambda w: pl.BlockSpec((1, ROW_TILE, w), lambda b, t: (b, t, 0))
    halo = lambda w: pl.BlockSpec((1, HALO, w), lambda b, t: (b, jnp.maximum(t * hpt - 1, 0), 0))
    full = lambda a: pl.BlockSpec(a.shape, lambda b, t: (0,) * a.ndim)
    return pl.pallas_call(
        _convpool_kernel,
        grid=(B, nt),
        in_specs=[row(768), halo(768), row(512), halo(512), full(dw), full(dwb), full(lng),
                  full(lnb), full(pw), full(pwb), full(plw), full(plb), full(pls)],
        out_specs=row(512),
        out_shape=jax.ShapeDtypeStruct((B, S, 512), F32),
        scratch_shapes=[pltpu.VMEM((HALO + ROW_TILE, CONV_W), F32),
                        pltpu.VMEM((HALO + ROW_TILE, POOL_W), F32)],
        compiler_params=pltpu.CompilerParams(
            dimension_semantics=("arbitrary", "arbitrary"), vmem_limit_bytes=VMEM_LIMIT),
        name="convpool",
    )(zc, zc, zp, zp, dw, dwb, lng, lnb, pw, pwb, plw, plb, pls)


def _attn_kernel(qi_ref, ws_ref, q_ref, ag_ref, kit_ref, kt_ref, v_ref, o_ref,
                 sc_ref, qm_ref, m_ref, l_ref, acc_ref, *, topk):
    j = pl.program_id(1)
    nfull = j // 2
    nch = nfull + 1
    Q = Q_BLOCK
    KC = KEY_CHUNK
    row = lax.broadcasted_iota(jnp.int32, (Q, 1), 0)
    qpos = j * Q + row
    lane128 = lax.broadcasted_iota(jnp.int32, (Q, 128), 1)

    qi = qi_ref[0]
    ws = ws_ref[0]
    lane256 = lax.broadcasted_iota(jnp.int32, (Q, 256), 1)
    qi_h = [jnp.where(lane256 // IDX_DIM == h, qi, 0.0) for h in range(IDX_HEADS)]
    w_h = [ws[:, IDX_DIM + h:IDX_DIM + h + 1] for h in range(IDX_HEADS)]

    def chunk_scores(c):
        kic = kit_ref[0, c]
        ki4 = jnp.concatenate([kic] * IDX_HEADS, axis=0)
        tot = jnp.zeros((Q, KC), F32)
        for h in range(IDX_HEADS):
            tot = tot + w_h[h] * jnp.maximum(_dot(qi_h[h], ki4, precision=HIGHEST), 0.0)
        return tot

    def fold(a, op):
        return op(a[:, 0:128], a[:, 128:256])

    def score_body(c, carry):
        mx, mn = carry
        s = chunk_scores(c)
        sc_ref[c] = s
        return jnp.maximum(mx, fold(s, jnp.maximum)), jnp.minimum(mn, fold(s, jnp.minimum))

    init = (jnp.full((Q, 128), -jnp.inf, F32), jnp.full((Q, 128), jnp.inf, F32))
    mx, mn = lax.fori_loop(0, nfull, score_body, init)
    s = chunk_scores(nfull)
    kpos = nfull * KC + lax.broadcasted_iota(jnp.int32, (Q, KC), 1)
    causal = kpos <= qpos
    sc_ref[nfull] = jnp.where(causal, s, -jnp.inf)
    mx = jnp.maximum(mx, fold(jnp.where(causal, s, -jnp.inf), jnp.maximum))
    mn = jnp.minimum(mn, fold(jnp.where(causal, s, jnp.inf), jnp.minimum))
    rowmax = jnp.max(mx, axis=1, keepdims=True)
    rowmin = jnp.min(mn, axis=1, keepdims=True)

    ncausal = (qpos + 1).astype(F32)
    kk = jnp.minimum(ncausal, float(topk))

    def count_ge(t):
        tb = jnp.broadcast_to(t, (Q, 128))

        def body(c, acc):
            sv = sc_ref[c]
            return (acc + jnp.where(sv[:, 0:128] >= tb, 1.0, 0.0)
                    + jnp.where(sv[:, 128:256] >= tb, 1.0, 0.0))

        acc = lax.fori_loop(0, nch, body, jnp.zeros((Q, 128), F32))
        return jnp.sum(acc, axis=1, keepdims=True)

    def bracket_ends(lo, hi):
        lob = jnp.broadcast_to(lo, (Q, 128))
        hib = jnp.broadcast_to(hi, (Q, 128))

        def body(c, carry):
            a, b = carry
            sv = sc_ref[c]
            for p in (sv[:, 0:128], sv[:, 128:256]):
                a = jnp.minimum(a, jnp.where(p >= lob, p, jnp.inf))
                b = jnp.maximum(b, jnp.where(p < hib, p, -jnp.inf))
            return a, b

        a, b = lax.fori_loop(0, nch, body, (jnp.full((Q, 128), jnp.inf, F32),
                                            jnp.full((Q, 128), -jnp.inf, F32)))
        return jnp.min(a, axis=1, keepdims=True), jnp.max(b, axis=1, keepdims=True)

    res0 = (ncausal <= kk).astype(F32)
    state0 = dict(lo=rowmin, hi=jnp.full((Q, 1), jnp.inf, F32), c_lo=ncausal,
                  c_hi=jnp.zeros((Q, 1), F32), res=res0, thr=rowmin,
                  need=jnp.full((Q, 1), NEED_ALL, F32), pivot=rowmax,
                  use_pivot=jnp.ones((Q, 1), F32))

    def bisect_pass(_, st):
        mid = jnp.where(st["use_pivot"] > 0, st["pivot"], 0.5 * (st["lo"] + st["hi"]))
        cnt = count_ge(mid)
        live = st["res"] == 0
        ge = cnt >= kk
        hit = live & (cnt == kk)
        up = live & ge
        dn = live & jnp.logical_not(ge)
        return dict(
            lo=jnp.where(up, mid, st["lo"]), c_lo=jnp.where(up, cnt, st["c_lo"]),
            hi=jnp.where(dn, mid, st["hi"]), c_hi=jnp.where(dn, cnt, st["c_hi"]),
            res=jnp.where(hit, 1.0, st["res"]), thr=jnp.where(hit, mid, st["thr"]),
            need=st["need"], pivot=st["pivot"], use_pivot=jnp.zeros((Q, 1), F32))

    def round_body(carry):
        it, st = carry
        st = lax.fori_loop(0, jnp.where(it == 0, 14, 2), bisect_pass, st)
        a, b = bracket_ends(st["lo"], st["hi"])
        live = st["res"] == 0
        tie = live & (a == b)
        st = dict(st)
        st["thr"] = jnp.where(tie, a, st["thr"])
        st["need"] = jnp.where(tie, kk - st["c_hi"], st["need"])
        st["res"] = jnp.where(tie, 1.0, st["res"])
        st["lo"] = jnp.where(live, a, st["lo"])
        st["pivot"] = b
        st["use_pivot"] = jnp.ones((Q, 1), F32)
        return it + 1, st

    def round_cond(carry):
        _, st = carry
        return jnp.min(st["res"]) == 0

    _, st = lax.while_loop(round_cond, round_body, (jnp.int32(0), state0))
    thr = jnp.broadcast_to(st["thr"], (Q, KC))
    need0 = st["need"]

    q = q_ref[0]
    for h in range(N_HEADS):
        g0 = (h // 4) * 256
        qm_ref[h] = jnp.where(lane256 // HEAD_DIM == h % 4, q[:, g0:g0 + 256], jnp.zeros((), BF16))
    m_ref[...] = jnp.full(m_ref.shape, MASK_BIAS, F32)
    l_ref[...] = jnp.zeros(l_ref.shape, F32)
    acc_ref[...] = jnp.zeros(acc_ref.shape, F32)
    ri = lax.broadcasted_iota(jnp.int32, (KC, KC), 0)
    ci = lax.broadcasted_iota(jnp.int32, (KC, KC), 1)
    tri = jnp.where(ri <= ci, 1.0, 0.0).astype(BF16)
    left = lane128 < HEAD_DIM

    def attn_body(c, need):
        sv = sc_ref[c]
        eq = sv == thr
        prefix = _dot(jnp.where(eq, 1.0, 0.0).astype(BF16), tri)
        take_tie = jnp.where(prefix <= need, 0.0, MASK_BIAS)
        bias = jnp.where(sv > thr, 0.0, jnp.where(eq, take_tie, MASK_BIAS))
        need = need - prefix[:, KC - 1:KC]
        k0 = pl.multiple_of(c * KC, KC)
        for pair in range(N_HEADS // 2):
            vp = v_ref[0, pl.ds(k0, KC), pair * 128:(pair + 1) * 128]
            pv = []
            al = []
            for h in (2 * pair, 2 * pair + 1):
                g0 = (h // 4) * 256
                logit = _dot(qm_ref[h], kt_ref[0, c, g0:g0 + 256, :]) + bias
                m_old = m_ref[h]
                m_new = jnp.maximum(m_old, jnp.max(logit, axis=1, keepdims=True))
                p = jnp.exp(logit - m_new)
                alpha = jnp.exp(m_old - m_new)
                l_ref[h] = alpha * l_ref[h] + jnp.sum(p, axis=1, keepdims=True)
                m_ref[h] = m_new
                pv.append(_dot(p.astype(BF16), vp))
                al.append(alpha)
            acc_ref[pair] = (acc_ref[pair] * jnp.where(left, al[0], al[1])
                             + jnp.where(left, pv[0], pv[1]))
        return need

    lax.fori_loop(0, nch, attn_body, need0)
    out = []
    for pair in range(N_HEADS // 2):
        out.append(acc_ref[pair] / jnp.where(left, l_ref[2 * pair], l_ref[2 * pair + 1]))
    o_ref[0] = jnp.concatenate(out, axis=1) * _silu(ag_ref[0])


def _attention(qi, ws, q, ag, kit, kt, v, topk):
    B, S, _ = q.shape
    nq = S // Q_BLOCK
    nch = S // KEY_CHUNK
    blk = lambda w: pl.BlockSpec((1, Q_BLOCK, w), lambda b, j: (b, j, 0))
    once = pl.Buffered(1)
    return pl.pallas_call(
        functools.partial(_attn_kernel, topk=topk),
        grid=(B, nq),
        in_specs=[
            blk(256), blk(128), blk(ATTN_W), blk(ATTN_W),
            pl.BlockSpec((1, nch, IDX_DIM, KEY_CHUNK), lambda b, j: (b, 0, 0, 0), pipeline_mode=once),
            pl.BlockSpec((1, nch, ATTN_W, KEY_CHUNK), lambda b, j: (b, 0, 0, 0), pipeline_mode=once),
            pl.BlockSpec((1, S, ATTN_W), lambda b, j: (b, 0, 0), pipeline_mode=once),
        ],
        out_specs=blk(ATTN_W),
        out_shape=jax.ShapeDtypeStruct((B, S, ATTN_W), F32),
        scratch_shapes=[
            pltpu.VMEM((nch, Q_BLOCK, KEY_CHUNK), F32),
            pltpu.VMEM((N_HEADS, Q_BLOCK, 256), BF16),
            pltpu.VMEM((N_HEADS, Q_BLOCK, 1), F32),
            pltpu.VMEM((N_HEADS, Q_BLOCK, 1), F32),
            pltpu.VMEM((N_HEADS // 2, Q_BLOCK, 128), F32),
        ],
        compiler_params=pltpu.CompilerParams(
            dimension_semantics=("arbitrary", "arbitrary"), vmem_limit_bytes=VMEM_LIMIT),
        name="attention",
    )(qi, ws, q, ag, kit, kt, v)


def _outproj_kernel(h_ref, yab_ref, yc_ref, p_ref, wo_ref, plew_ref, pleg_ref, fng_ref, o_ref,
                    *, final):
    h = (h_ref[0] + _dot(yab_ref[0].astype(BF16), wo_ref[0:512, :])
         + _dot(yc_ref[0].astype(BF16), wo_ref[512:1024, :]))
    ple = _dot(p_ref[0].astype(BF16), plew_ref[...])
    h = h + ple * _sigmoid(_dot(h.astype(BF16), pleg_ref[...]))
    if final:
        h = h * lax.rsqrt(jnp.mean(h * h, axis=-1, keepdims=True) + EPS) * fng_ref[...]
    o_ref[0] = h


def _outproj(h, yab, yc, p, wo, plew, pleg, fng, final):
    B, S, D = h.shape
    nt = S // ROW_TILE
    row = lambda w: pl.BlockSpec((1, ROW_TILE, w), lambda b, t: (b, t, 0))
    full = lambda a: pl.BlockSpec(a.shape, lambda b, t: (0,) * a.ndim)
    return pl.pallas_call(
        functools.partial(_outproj_kernel, final=final),
        grid=(B, nt),
        in_specs=[row(D), row(512), row(512), row(p.shape[-1]), full(wo), full(plew), full(pleg),
                  full(fng)],
        out_specs=row(D),
        out_shape=jax.ShapeDtypeStruct((B, S, D), F32),
        compiler_params=pltpu.CompilerParams(
            dimension_semantics=("arbitrary", "arbitrary"), vmem_limit_bytes=VMEM_LIMIT),
        name="outproj",
    )(h, yab, yc, p, wo, plew, pleg, fng)


def _block_diag(w):
    g, a, b = w.shape
    out = jnp.zeros((g * a, g * b), w.dtype)
    for i in range(g):
        out = out.at[i * a:(i + 1) * a, i * b:(i + 1) * b].set(w[i])
    return out


def kernel(x, p, norm_g, w_in, b_in, conv_dw_w, conv_dw_b, conv_ln_g, conv_ln_b, conv_pw_w,
           conv_pw_b, pool_w, pool_b, pool_scale, w_out, ple_w, ple_gate_w, final_norm_g):
    B, S, D = x.shape
    depth = w_in.shape[0]
    assert S % ROW_TILE == 0 and D == 1024 and w_in.shape[2] == N_MAIN + 324
    topk = min(TOPK_MAX, S // 4)

    half = HEAD_DIM // 2
    inv = ROPE_THETA ** (-jnp.arange(half, dtype=F32) / half)
    ang = jnp.arange(S).astype(F32)[:, None] * inv[None, :]
    cos = jnp.tile(jnp.cos(ang), (1, 4))
    sin = jnp.tile(jnp.concatenate([-jnp.sin(ang), jnp.sin(ang)], axis=1), (1, 2))

    row2 = lambda a: a.reshape(1, -1)
    h = x
    for i in range(depth):
        wm = w_in[i, :, :N_MAIN].astype(BF16)
        bm = row2(b_in[i, :N_MAIN])
        wx = jnp.pad(w_in[i, :, N_MAIN:], ((0, 0), (0, IDX_PAD - 324)))
        bx = row2(jnp.pad(b_in[i, N_MAIN:], (0, IDX_PAD - 324)))
        zc, zp, q, kt, v, ag, qi, kit, ws = _inproj(h, row2(norm_g[i]), wm, bm, wx, bx, cos, sin)
        yab = _convpool(zc, zp, conv_dw_w[i], row2(conv_dw_b[i]), row2(conv_ln_g[i]),
                        row2(conv_ln_b[i]), conv_pw_w[i].astype(BF16), row2(conv_pw_b[i]),
                        _block_diag(pool_w[i]).astype(BF16), row2(pool_b[i]), row2(pool_scale[i]))
        yc = _attention(qi, ws, q, ag, kit, kt, v, topk)
        h = _outproj(h, yab, yc, p[i], w_out[i].astype(BF16), ple_w[i].astype(BF16),
                     ple_gate_w[i].astype(BF16), row2(final_norm_g), final=(i == depth - 1))
    return h
```

```python
import functools

import jax
import jax.numpy as jnp
from jax import lax
from jax.experimental import pallas as pl
from jax.experimental.pallas import tpu as pltpu

F32 = jnp.float32
BF16 = jnp.bfloat16

CONV_W = 256
CONV_K = 31
POOL_W = 256
POOL_GW = 64
POOL_WINDOWS = (2, 4, 8, 16)
HEAD_DIM = 64
ATTN_W = 512
N_HEADS = 8
IDX_HEADS = 4
IDX_DIM = 64
TOPK_MAX = 256
ROPE_THETA = 10000.0
EPS = 1e-6

SUBLANES = 8
Q_BLOCK = 128
KEY_CHUNK = 256
SCAN_ROWS = 2 * KEY_CHUNK
LOG2E = 1.4426950408889634
HALO = 32
ROW_TILE = 512
IDX_PAD = 384
N_MAIN = 3 * CONV_W + 2 * POOL_W + 4 * ATTN_W
MASK_BIAS = -1e30
NEED_ALL = 1e9
VMEM_LIMIT = 60 * 1024 * 1024
HIGHEST = lax.Precision.HIGHEST


def _sigmoid(x):
    return 1.0 / (1.0 + jnp.exp(-x))


def _silu(x):
    return x * _sigmoid(x)


def _dot(a, b, precision=None):
    return lax.dot_general(a, b, (((1,), (0,)), ((), ())), precision=precision,
                           preferred_element_type=F32)


def _split_bf16(x):
    hi = x.astype(BF16)
    return hi, (x - hi.astype(F32)).astype(BF16)


def _rope_slab(x, cos, sin):
    lane = lax.broadcasted_iota(jnp.int32, x.shape, 1)
    swapped = jnp.where((lane & 32) == 0, pltpu.roll(x, 96, 1), pltpu.roll(x, 32, 1))
    return x * cos + swapped * sin


def _rope(x, cos, sin):
    return jnp.concatenate(
        [_rope_slab(x[:, s:s + 128], cos, sin) for s in range(0, x.shape[1], 128)], axis=1)


def _inproj_kernel(h_ref, g_ref, wm_ref, bm_ref, wx_ref, bx_ref, cos_ref, sin_ref,
                   zc_ref, zp_ref, qt_ref, k_ref, vt_ref, ag_ref, qih_ref, qil_ref, wsh_ref, wsl_ref,
                   wt_ref):
    x = h_ref[0]
    n = x * lax.rsqrt(jnp.mean(x * x, axis=-1, keepdims=True) + EPS) * g_ref[...]
    nb = n.astype(BF16)
    cos = cos_ref[...]
    sin = sin_ref[...]

    def proj(a, b):
        return _dot(nb, wm_ref[:, a:b]) + bm_ref[:, a:b]

    zc_ref[0] = proj(0, 768)
    zp_ref[0] = proj(768, 1280)
    qt = (_rope(proj(1280, 1792), cos, sin) * (HEAD_DIM ** -0.5 * LOG2E)).T.astype(BF16)
    for jb in range(ROW_TILE // Q_BLOCK):
        qt_ref[0, jb] = qt[:, jb * Q_BLOCK:(jb + 1) * Q_BLOCK]
    k_ref[0] = _rope(proj(1792, 2304), cos, sin).astype(BF16)
    vt = proj(2304, 2816).T.astype(BF16)
    for c in range(ROW_TILE // KEY_CHUNK):
        vt_ref[0, c] = vt[:, c * KEY_CHUNK:(c + 1) * KEY_CHUNK]
    ag_ref[0] = proj(2816, 3328)

    zx = _dot(n, wx_ref[...], precision=HIGHEST) + bx_ref[...]
    qit = (_rope(zx[:, 0:256], cos, sin) * (IDX_DIM ** -0.5)).T
    qit_hi, qit_lo = _split_bf16(qit)
    pad = jnp.zeros((128 - IDX_DIM, Q_BLOCK), BF16)
    for jb in range(ROW_TILE // Q_BLOCK):
        for part, ref in ((qit_hi, qih_ref), (qit_lo, qil_ref)):
            ref[0, jb] = jnp.concatenate(
                [jnp.concatenate([part[h * IDX_DIM:(h + 1) * IDX_DIM, jb * Q_BLOCK:(jb + 1) * Q_BLOCK],
                                  pad], axis=0) for h in range(IDX_HEADS)], axis=1)
    tail = zx[:, 256:384]
    lane = lax.broadcasted_iota(jnp.int32, tail.shape, 1)
    tail = jnp.where(lane < IDX_DIM, _rope_slab(tail, cos, sin), tail)
    tail_hi, tail_lo = _split_bf16(tail)
    wsh_ref[0] = tail_hi
    wsl_ref[0] = tail_lo
    wt = tail.T[IDX_DIM:IDX_DIM + SUBLANES, :]
    for jb in range(ROW_TILE // Q_BLOCK):
        wt_ref[0, jb] = wt[:, jb * Q_BLOCK:(jb + 1) * Q_BLOCK]


def _inproj(h, g, wm, bm, wx, bx, cos, sin):
    B, S, D = h.shape
    nt = S // ROW_TILE
    cpt = ROW_TILE // KEY_CHUNK
    qpt = ROW_TILE // Q_BLOCK
    nch = S // KEY_CHUNK
    nq = S // Q_BLOCK
    row = lambda w: pl.BlockSpec((1, ROW_TILE, w), lambda b, t: (b, t, 0))
    full = lambda a: pl.BlockSpec(a.shape, lambda b, t: (0,) * a.ndim)
    tab = pl.BlockSpec((ROW_TILE, 128), lambda b, t: (t, 0))
    qblk = lambda r, w: pl.BlockSpec((1, qpt, r, w), lambda b, t: (b, t, 0, 0))
    out_shape = (
        jax.ShapeDtypeStruct((B, S, 768), F32),
        jax.ShapeDtypeStruct((B, S, 512), F32),
        jax.ShapeDtypeStruct((B, nq, ATTN_W, Q_BLOCK), BF16),
        jax.ShapeDtypeStruct((B, S, ATTN_W), BF16),
        jax.ShapeDtypeStruct((B, nch, ATTN_W, KEY_CHUNK), BF16),
        jax.ShapeDtypeStruct((B, S, ATTN_W), F32),
        jax.ShapeDtypeStruct((B, nq, 128, IDX_HEADS * Q_BLOCK), BF16),
        jax.ShapeDtypeStruct((B, nq, 128, IDX_HEADS * Q_BLOCK), BF16),
        jax.ShapeDtypeStruct((B, S, 128), BF16),
        jax.ShapeDtypeStruct((B, S, 128), BF16),
        jax.ShapeDtypeStruct((B, nq, SUBLANES, Q_BLOCK), F32),
    )
    out_specs = (
        row(768), row(512), qblk(ATTN_W, Q_BLOCK), row(ATTN_W),
        pl.BlockSpec((1, cpt, ATTN_W, KEY_CHUNK), lambda b, t: (b, t, 0, 0)),
        row(ATTN_W), qblk(128, IDX_HEADS * Q_BLOCK), qblk(128, IDX_HEADS * Q_BLOCK),
        row(128), row(128), qblk(SUBLANES, Q_BLOCK),
    )
    return pl.pallas_call(
        _inproj_kernel,
        grid=(B, nt),
        in_specs=[row(D), full(g), full(wm), full(bm), full(wx), full(bx), tab, tab],
        out_specs=out_specs,
        out_shape=out_shape,
        compiler_params=pltpu.CompilerParams(
            dimension_semantics=("arbitrary", "arbitrary"), vmem_limit_bytes=VMEM_LIMIT),
        name="inproj",
    )(h, g, wm, bm, wx, bx, cos, sin)


def _convpool_kernel(zc_ref, zch_ref, zp_ref, zph_ref, dw_ref, dwb_ref, lng_ref, lnb_ref,
                     pw_ref, pwb_ref, plw_ref, plb_ref, pls_ref, o_ref, ubuf, xbuf):
    t = pl.program_id(1)
    keep = (t > 0).astype(F32)

    zc = zc_ref[0]
    zch = zch_ref[0]
    ubuf[0:HALO] = zch[:, 0:256] * _sigmoid(zch[:, 256:512]) * keep
    ubuf[HALO:HALO + ROW_TILE] = zc[:, 0:256] * _sigmoid(zc[:, 256:512])
    acc = jnp.zeros((ROW_TILE, CONV_W), F32) + dwb_ref[...]
    for j in range(CONV_K):
        acc = acc + ubuf[pl.ds(HALO - (CONV_K - 1) + j, ROW_TILE), :] * dw_ref[j:j + 1, :]
    mu = jnp.mean(acc, axis=-1, keepdims=True)
    xc = acc - mu
    y = xc * lax.rsqrt(jnp.mean(xc * xc, axis=-1, keepdims=True) + EPS) * lng_ref[...] + lnb_ref[...]
    y = _silu(y)
    ya = (_dot(y.astype(BF16), pw_ref[...]) + pwb_ref[...]) * _silu(zc[:, 512:768])

    zp = zp_ref[0]
    x = zp[:, 0:256]
    xbuf[0:HALO] = zph_ref[0][:, 0:256] * keep
    xbuf[HALO:HALO + ROW_TILE] = x
    run = x
    sums = {}
    for d in range(1, POOL_WINDOWS[-1]):
        run = run + xbuf[pl.ds(HALO - d, ROW_TILE), :]
        if d + 1 in POOL_WINDOWS:
            sums[d + 1] = run
    lane = lax.broadcasted_iota(jnp.int32, (ROW_TILE, POOL_W), 1)
    grp = lane // POOL_GW
    wsum = jnp.where(grp == 0, sums[2], jnp.where(grp == 1, sums[4],
                                                  jnp.where(grp == 2, sums[8], sums[16])))
    win = jnp.where(grp == 0, 2, jnp.where(grp == 1, 4, jnp.where(grp == 2, 8, 16)))
    pos = t * ROW_TILE + lax.broadcasted_iota(jnp.int32, (ROW_TILE, POOL_W), 0)
    cnt = jnp.minimum(pos + 1, win).astype(F32)
    d = wsum / cnt - x
    yb = (_dot(d.astype(BF16), plw_ref[...]) + plb_ref[...]) * pls_ref[...] * _silu(zp[:, 256:512])

    o_ref[0] = jnp.concatenate([ya, yb], axis=1)


def _convpool(zc, zp, dw, dwb, lng, lnb, pw, pwb, plw, plb, pls):
    B, S, _ = zc.shape
    nt = S // ROW_TILE
    hpt = ROW_TILE // HALO
    row = lambda w: pl.BlockSpec((1, ROW_TILE, w), lambda b, t: (b, t, 0))
    halo = lambda w: pl.BlockSpec((1, HALO, w), lambda b, t: (b, jnp.maximum(t * hpt - 1, 0), 0))
    full = lambda a: pl.BlockSpec(a.shape, lambda b, t: (0,) * a.ndim)
    return pl.pallas_call(
        _convpool_kernel,
        grid=(B, nt),
        in_specs=[row(768), halo(768), row(512), halo(512), full(dw), full(dwb), full(lng),
                  full(lnb), full(pw), full(pwb), full(plw), full(plb), full(pls)],
        out_specs=row(512),
        out_shape=jax.ShapeDtypeStruct((B, S, 512), F32),
        scratch_shapes=[pltpu.VMEM((HALO + ROW_TILE, CONV_W), F32),
                        pltpu.VMEM((HALO + ROW_TILE, POOL_W), F32)],
        compiler_params=pltpu.CompilerParams(
            dimension_semantics=("arbitrary", "arbitrary"), vmem_limit_bytes=VMEM_LIMIT),
        name="convpool",
    )(zc, zc, zp, zp, dw, dwb, lng, lnb, pw, pwb, plw, plb, pls)


def _fold_rows(x, op, fn=lambda p: p):
    parts = [fn(x[r:r + SUBLANES, :]) for r in range(0, x.shape[0], SUBLANES)]
    while len(parts) > 1:
        parts = [op(parts[i], parts[i + 1]) for i in range(0, len(parts), 2)]
    return parts[0]


def _attn_kernel(qt_ref, qih_ref, qil_ref, wt_ref, ag_ref, wsh_ref, wsl_ref, k_ref, vt_ref, o_ref,
                 sc_ref, qp_ref, acc_ref, lg_ref, p_ref, *, topk):
    j = pl.program_id(1)
    nfull = j // 2
    nch = nfull + 1
    Q = Q_BLOCK
    KC = KEY_CHUNK
    qpos = j * Q + lax.broadcasted_iota(jnp.int32, (1, Q), 1)

    qh = qih_ref[0, 0]
    ql = qil_ref[0, 0]
    wt = wt_ref[0, 0]
    w_h = [wt[h:h + 1, :] for h in range(IDX_HEADS)]

    def chunk_scores(c):
        r0 = pl.multiple_of(c * KC, KC)
        a_hi = wsh_ref[0, pl.ds(r0, KC), :]
        a_lo = wsl_ref[0, pl.ds(r0, KC), :]
        s = _dot(a_hi, qh) + _dot(a_hi, ql) + _dot(a_lo, qh)
        tot = jnp.zeros((KC, Q), F32)
        for h in range(IDX_HEADS):
            tot = tot + w_h[h] * jnp.maximum(s[:, h * Q:(h + 1) * Q], 0.0)
        return tot

    def score_body(c, carry):
        mx, mn = carry
        s = chunk_scores(c)
        sc_ref[pl.ds(pl.multiple_of(c * KC, KC), KC), :] = s
        return jnp.maximum(mx, _fold_rows(s, jnp.maximum)), jnp.minimum(mn, _fold_rows(s, jnp.minimum))

    init = (jnp.full((SUBLANES, Q), -jnp.inf, F32), jnp.full((SUBLANES, Q), jnp.inf, F32))
    mx, mn = lax.fori_loop(0, nfull, score_body, init)
    s = chunk_scores(nfull)
    kpos = nfull * KC + lax.broadcasted_iota(jnp.int32, (KC, Q), 0)
    causal = kpos <= qpos
    s_lo = jnp.where(causal, s, -jnp.inf)
    sc_ref[pl.ds(pl.multiple_of(nfull * KC, KC), KC), :] = s_lo
    mx = jnp.maximum(mx, _fold_rows(s_lo, jnp.maximum))
    mn = jnp.minimum(mn, _fold_rows(jnp.where(causal, s, jnp.inf), jnp.minimum))
    rowmax = jnp.max(mx, axis=0, keepdims=True)
    rowmin = jnp.min(mn, axis=0, keepdims=True)

    @pl.when(nch % 2 == 1)
    def _():
        sc_ref[pl.ds(pl.multiple_of(nch * KC, KC), KC), :] = jnp.full((KC, Q), -jnp.inf, F32)
    nscan = (nch + 1) // 2

    ncausal = (qpos + 1).astype(F32)
    kk = jnp.minimum(ncausal, float(topk))

    def count_ge(t):
        tb = jnp.broadcast_to(t, (SUBLANES, Q))

        def body(c, acc):
            sv = sc_ref[pl.ds(pl.multiple_of(c * SCAN_ROWS, SCAN_ROWS), SCAN_ROWS), :]
            return acc + _fold_rows(sv, jnp.add, lambda p: jnp.where(p >= tb, 1.0, 0.0))

        acc = lax.fori_loop(0, nscan, body, jnp.zeros((SUBLANES, Q), F32))
        return jnp.sum(acc, axis=0, keepdims=True)

    def bracket_ends(lo, hi):
        lob = jnp.broadcast_to(lo, (SUBLANES, Q))
        hib = jnp.broadcast_to(hi, (SUBLANES, Q))

        def body(c, carry):
            a, b = carry
            sv = sc_ref[pl.ds(pl.multiple_of(c * SCAN_ROWS, SCAN_ROWS), SCAN_ROWS), :]
            a = jnp.minimum(a, _fold_rows(sv, jnp.minimum, lambda p: jnp.where(p >= lob, p, jnp.inf)))
            b = jnp.maximum(b, _fold_rows(sv, jnp.maximum, lambda p: jnp.where(p < hib, p, -jnp.inf)))
            return a, b

        a, b = lax.fori_loop(0, nscan, body, (jnp.full((SUBLANES, Q), jnp.inf, F32),
                                            jnp.full((SUBLANES, Q), -jnp.inf, F32)))
        return jnp.min(a, axis=0, keepdims=True), jnp.max(b, axis=0, keepdims=True)

    res0 = (ncausal <= kk).astype(F32)
    state0 = dict(lo=rowmin, hi=jnp.full((1, Q), jnp.inf, F32), c_hi=jnp.zeros((1, Q), F32),
                  res=res0, thr=rowmin, need=jnp.full((1, Q), NEED_ALL, F32), pivot=rowmax,
                  use_pivot=jnp.ones((1, Q), F32))

    def bisect_pass(_, st):
        mid = jnp.where(st["use_pivot"] > 0, st["pivot"], 0.5 * (st["lo"] + st["hi"]))
        cnt = count_ge(mid)
        live = st["res"] == 0
        ge = cnt >= kk
        hit = live & (cnt == kk)
        up = live & ge
        dn = live & jnp.logical_not(ge)
        return dict(
            lo=jnp.where(up, mid, st["lo"]),
            hi=jnp.where(dn, mid, st["hi"]), c_hi=jnp.where(dn, cnt, st["c_hi"]),
            res=jnp.where(hit, 1.0, st["res"]), thr=jnp.where(hit, mid, st["thr"]),
            need=st["need"], pivot=st["pivot"], use_pivot=jnp.zeros((1, Q), F32))

    def round_body(carry):
        it, st = carry
        st = lax.fori_loop(0, jnp.where(it == 0, 14, 2), bisect_pass, st)
        a, b = bracket_ends(st["lo"], st["hi"])
        live = st["res"] == 0
        tie = live & (a == b)
        st = dict(st)
        st["thr"] = jnp.where(tie, a, st["thr"])
        st["need"] = jnp.where(tie, kk - st["c_hi"], st["need"])
        st["res"] = jnp.where(tie, 1.0, st["res"])
        st["lo"] = jnp.where(live, a, st["lo"])
        st["pivot"] = b
        st["use_pivot"] = jnp.ones((1, Q), F32)
        return it + 1, st

    def round_cond(carry):
        _, st = carry
        return jnp.min(st["res"]) == 0

    _, st = lax.while_loop(round_cond, round_body, (jnp.int32(0), state0))
    thr = st["thr"]
    need0 = st["need"]

    qt = qt_ref[0, 0]
    rowhead = lax.broadcasted_iota(jnp.int32, (256, Q), 0) // HEAD_DIM
    zero = jnp.zeros((), BF16)
    for pr in range(N_HEADS // 2):
        qg = qt[(pr // 2) * 256:(pr // 2 + 1) * 256, :]
        qp_ref[pr] = jnp.concatenate([jnp.where(rowhead == (2 * pr) % 4, qg, zero),
                                      jnp.where(rowhead == (2 * pr + 1) % 4, qg, zero)], axis=1)
    acc_ref[...] = jnp.zeros(acc_ref.shape, F32)
    ri = lax.broadcasted_iota(jnp.int32, (KC, KC), 0)
    ci = lax.broadcasted_iota(jnp.int32, (KC, KC), 1)
    tril = jnp.where(ci <= ri, 1.0, 0.0).astype(BF16)

    def attn_body(c, carry):
        need, ms, ls = carry
        r0 = pl.multiple_of(c * KC, KC)
        sv = sc_ref[pl.ds(r0, KC), :]
        eq = sv == thr
        prefix = _dot(tril, jnp.where(eq, 1.0, 0.0).astype(BF16))
        take_tie = jnp.where(prefix <= need, 0.0, MASK_BIAS)
        bias = jnp.where(sv > thr, 0.0, jnp.where(eq, take_tie, MASK_BIAS))
        need = need - prefix[KC - 1:KC, :]
        bias2 = jnp.concatenate([bias, bias], axis=1)
        new_ms, new_ls = [], []
        for pr in range(N_HEADS // 2):
            g0 = (pr // 2) * 256
            lg_ref[pr] = _dot(k_ref[0, pl.ds(r0, KC), g0:g0 + 256], qp_ref[pr]) + bias2
        for pr in range(N_HEADS // 2):
            m_new = jnp.maximum(ms[pr], jnp.max(_fold_rows(lg_ref[pr], jnp.maximum), axis=0,
                                                keepdims=True))
            alpha = jnp.exp2(ms[pr] - m_new)
            sums = []
            for r in range(0, KC, 2 * SUBLANES):
                pe = jnp.exp2(lg_ref[pr, r:r + 2 * SUBLANES, :] - m_new)
                p_ref[pr, r:r + 2 * SUBLANES, :] = pe.astype(BF16)
                sums.append(pe[0:SUBLANES, :] + pe[SUBLANES:2 * SUBLANES, :])
            while len(sums) > 1:
                sums = [sums[i] + sums[i + 1] for i in range(0, len(sums), 2)]
            new_ls.append(alpha * ls[pr] + jnp.sum(sums[0], axis=0, keepdims=True))
            new_ms.append(m_new)
            o = _dot(vt_ref[0, c, pr * 128:(pr + 1) * 128, :], p_ref[pr])
            a0 = pr * 128
            acc_ref[a0:a0 + 64, :] = acc_ref[a0:a0 + 64, :] * alpha[:, 0:Q] + o[0:64, 0:Q]
            acc_ref[a0 + 64:a0 + 128, :] = (acc_ref[a0 + 64:a0 + 128, :] * alpha[:, Q:2 * Q]
                                            + o[64:128, Q:2 * Q])
        return need, tuple(new_ms), tuple(new_ls)

    ms0 = tuple(jnp.full((1, 2 * Q), MASK_BIAS, F32) for _ in range(N_HEADS // 2))
    ls0 = tuple(jnp.zeros((1, 2 * Q), F32) for _ in range(N_HEADS // 2))
    _, _, ls = lax.fori_loop(0, nch, attn_body, (need0, ms0, ls0))
    outs = []
    for pr in range(N_HEADS // 2):
        a0 = pr * 128
        outs.append(acc_ref[a0:a0 + 64, :] / ls[pr][:, 0:Q])
        outs.append(acc_ref[a0 + 64:a0 + 128, :] / ls[pr][:, Q:2 * Q])
    o_ref[0] = jnp.concatenate(outs, axis=0).T * _silu(ag_ref[0])


def _attention(qt, qih, qil, wt, ag, wsh, wsl, k, vt, topk):
    B, S, _ = k.shape
    nq = S // Q_BLOCK
    nch = S // KEY_CHUNK
    qblk = lambda r, w: pl.BlockSpec((1, 1, r, w), lambda b, j: (b, j, 0, 0))
    once = pl.Buffered(1)
    seq = lambda w: pl.BlockSpec((1, S, w), lambda b, j: (b, 0, 0), pipeline_mode=once)
    return pl.pallas_call(
        functools.partial(_attn_kernel, topk=topk),
        grid=(B, nq),
        in_specs=[
            qblk(ATTN_W, Q_BLOCK), qblk(128, IDX_HEADS * Q_BLOCK), qblk(128, IDX_HEADS * Q_BLOCK),
            qblk(SUBLANES, Q_BLOCK),
            pl.BlockSpec((1, Q_BLOCK, ATTN_W), lambda b, j: (b, j, 0)),
            seq(128), seq(128), seq(ATTN_W),
            pl.BlockSpec((1, nch, ATTN_W, KEY_CHUNK), lambda b, j: (b, 0, 0, 0), pipeline_mode=once),
        ],
        out_specs=pl.BlockSpec((1, Q_BLOCK, ATTN_W), lambda b, j: (b, j, 0)),
        out_shape=jax.ShapeDtypeStruct((B, S, ATTN_W), F32),
        scratch_shapes=[
            pltpu.VMEM((S, Q_BLOCK), F32),
            pltpu.VMEM((N_HEADS // 2, 256, 2 * Q_BLOCK), BF16),
            pltpu.VMEM((ATTN_W, Q_BLOCK), F32),
            pltpu.VMEM((N_HEADS // 2, KEY_CHUNK, 2 * Q_BLOCK), F32),
            pltpu.VMEM((N_HEADS // 2, KEY_CHUNK, 2 * Q_BLOCK), BF16),
        ],
        compiler_params=pltpu.CompilerParams(
            dimension_semantics=("arbitrary", "arbitrary"), vmem_limit_bytes=VMEM_LIMIT),
        name="attention",
    )(qt, qih, qil, wt, ag, wsh, wsl, k, vt)


def _outproj_kernel(h_ref, yab_ref, yc_ref, p_ref, wo_ref, plew_ref, pleg_ref, fng_ref, o_ref,
                    *, final):
    h = (h_ref[0] + _dot(yab_ref[0].astype(BF16), wo_ref[0:512, :])
         + _dot(yc_ref[0].astype(BF16), wo_ref[512:1024, :]))
    ple = _dot(p_ref[0].astype(BF16), plew_ref[...])
    h = h + ple * _sigmoid(_dot(h.astype(BF16), pleg_ref[...]))
    if final:
        h = h * lax.rsqrt(jnp.mean(h * h, axis=-1, keepdims=True) + EPS) * fng_ref[...]
    o_ref[0] = h


def _outproj(h, yab, yc, p, wo, plew, pleg, fng, final):
    B, S, D = h.shape
    nt = S // ROW_TILE
    row = lambda w: pl.BlockSpec((1, ROW_TILE, w), lambda b, t: (b, t, 0))
    full = lambda a: pl.BlockSpec(a.shape, lambda b, t: (0,) * a.ndim)
    return pl.pallas_call(
        functools.partial(_outproj_kernel, final=final),
        grid=(B, nt),
        in_specs=[row(D), row(512), row(512), row(p.shape[-1]), full(wo), full(plew), full(pleg),
                  full(fng)],
        out_specs=row(D),
        out_shape=jax.ShapeDtypeStruct((B, S, D), F32),
        compiler_params=pltpu.CompilerParams(
            dimension_semantics=("arbitrary", "arbitrary"), vmem_limit_bytes=VMEM_LIMIT),
        name="outproj",
    )(h, yab, yc, p, wo, plew, pleg, fng)


def _block_diag(w):
    g, a, b = w.shape
    out = jnp.zeros((g * a, g * b), w.dtype)
    for i in range(g):
        out = out.at[i * a:(i + 1) * a, i * b:(i + 1) * b].set(w[i])
    return out


def kernel(x, p, norm_g, w_in, b_in, conv_dw_w, conv_dw_b, conv_ln_g, conv_ln_b, conv_pw_w,
           conv_pw_b, pool_w, pool_b, pool_scale, w_out, ple_w, ple_gate_w, final_norm_g):
    B, S, D = x.shape
    depth = w_in.shape[0]
    assert S % ROW_TILE == 0 and D == 1024 and w_in.shape[2] == N_MAIN + 324
    topk = min(TOPK_MAX, S // 4)

    half = HEAD_DIM // 2
    inv = ROPE_THETA ** (-jnp.arange(half, dtype=F32) / half)
    ang = jnp.arange(S).astype(F32)[:, None] * inv[None, :]
    cos = jnp.tile(jnp.cos(ang), (1, 4))
    sin = jnp.tile(jnp.concatenate([-jnp.sin(ang), jnp.sin(ang)], axis=1), (1, 2))

    row2 = lambda a: a.reshape(1, -1)
    h = x
    for i in range(depth):
        wm = w_in[i, :, :N_MAIN].astype(BF16)
        bm = row2(b_in[i, :N_MAIN])
        wx = jnp.pad(w_in[i, :, N_MAIN:], ((0, 0), (0, IDX_PAD - 324)))
        bx = row2(jnp.pad(b_in[i, N_MAIN:], (0, IDX_PAD - 324)))
        zc, zp, qt, k, vt, ag, qih, qil, wsh, wsl, wt = _inproj(
            h, row2(norm_g[i]), wm, bm, wx, bx, cos, sin)
        yab = _convpool(zc, zp, conv_dw_w[i], row2(conv_dw_b[i]), row2(conv_ln_g[i]),
                        row2(conv_ln_b[i]), conv_pw_w[i].astype(BF16), row2(conv_pw_b[i]),
                        _block_diag(pool_w[i]).astype(BF16), row2(pool_b[i]), row2(pool_scale[i]))
        yc = _attention(qt, qih, qil, wt, ag, wsh, wsl, k, vt, topk)
        h = _outproj(h, yab, yc, p[i], w_out[i].astype(BF16), ple_w[i].astype(BF16),
                     ple_gate_w[i].astype(BF16), row2(final_norm_g), final=(i == depth - 1))
    return h
```

```python
import functools

import jax
import jax.numpy as jnp
from jax import lax
from jax.experimental import pallas as pl
from jax.experimental.pallas import tpu as pltpu

F32 = jnp.float32
BF16 = jnp.bfloat16

CONV_W = 256
CONV_K = 31
POOL_W = 256
POOL_GW = 64
POOL_WINDOWS = (2, 4, 8, 16)
HEAD_DIM = 64
ATTN_W = 512
N_HEADS = 8
IDX_HEADS = 4
IDX_DIM = 64
TOPK_MAX = 256
ROPE_THETA = 10000.0
EPS = 1e-6

SUBLANES = 8
Q_BLOCK = 128
KEY_CHUNK = 256
SCAN_ROWS = 2 * KEY_CHUNK
BIAS_UNROLL = 4
FIRST_PASSES = 18
NEXT_PASSES = 2
LOG2E = 1.4426950408889634
HALO = 32
ROW_TILE = 512
IDX_PAD = 384
N_MAIN = 3 * CONV_W + 2 * POOL_W + 4 * ATTN_W
MASK_BIAS = -1e30
NEED_ALL = 1e9
VMEM_LIMIT = 60 * 1024 * 1024
HIGHEST = lax.Precision.HIGHEST


def _sigmoid(x):
    return 1.0 / (1.0 + jnp.exp(-x))


def _silu(x):
    return x * _sigmoid(x)


def _dot(a, b, precision=None):
    return lax.dot_general(a, b, (((1,), (0,)), ((), ())), precision=precision,
                           preferred_element_type=F32)


def _rope_slab(x, cos, sin):
    lane = lax.broadcasted_iota(jnp.int32, x.shape, 1)
    swapped = jnp.where((lane & 32) == 0, pltpu.roll(x, 96, 1), pltpu.roll(x, 32, 1))
    return x * cos + swapped * sin


def _rope(x, cos, sin):
    return jnp.concatenate(
        [_rope_slab(x[:, s:s + 128], cos, sin) for s in range(0, x.shape[1], 128)], axis=1)


def _inproj_kernel(h_ref, g_ref, wm_ref, bm_ref, wx_ref, bx_ref, cos_ref, sin_ref,
                   zc_ref, zp_ref, qt_ref, k_ref, vt_ref, ag_ref, qi_ref, ws_ref, wt_ref):
    x = h_ref[0]
    n = x * lax.rsqrt(jnp.mean(x * x, axis=-1, keepdims=True) + EPS) * g_ref[...]
    nb = n.astype(BF16)
    cos = cos_ref[...]
    sin = sin_ref[...]

    def proj(a, b):
        return _dot(nb, wm_ref[:, a:b]) + bm_ref[:, a:b]

    zc_ref[0] = proj(0, 768)
    zp_ref[0] = proj(768, 1280)
    qt = (_rope(proj(1280, 1792), cos, sin) * (HEAD_DIM ** -0.5 * LOG2E)).T.astype(BF16)
    for jb in range(ROW_TILE // Q_BLOCK):
        qt_ref[0, jb] = qt[:, jb * Q_BLOCK:(jb + 1) * Q_BLOCK]
    k_ref[0] = _rope(proj(1792, 2304), cos, sin).astype(BF16)
    vt = proj(2304, 2816).T.astype(BF16)
    for c in range(ROW_TILE // KEY_CHUNK):
        vt_ref[0, c] = vt[:, c * KEY_CHUNK:(c + 1) * KEY_CHUNK]
    ag_ref[0] = proj(2816, 3328)

    zx = _dot(n, wx_ref[...], precision=HIGHEST) + bx_ref[...]
    qit = (_rope(zx[:, 0:256], cos, sin) * (IDX_DIM ** -0.5)).T.astype(BF16)
    pad = jnp.zeros((128 - IDX_DIM, Q_BLOCK), BF16)
    for jb in range(ROW_TILE // Q_BLOCK):
        qi_ref[0, jb] = jnp.concatenate(
            [jnp.concatenate([qit[h * IDX_DIM:(h + 1) * IDX_DIM, jb * Q_BLOCK:(jb + 1) * Q_BLOCK],
                              pad], axis=0) for h in range(IDX_HEADS)], axis=1)
    tail = zx[:, 256:384]
    lane = lax.broadcasted_iota(jnp.int32, tail.shape, 1)
    tail = jnp.where(lane < IDX_DIM, _rope_slab(tail, cos, sin), tail)
    ws_ref[0] = tail.astype(BF16)
    wt = tail.T[IDX_DIM:IDX_DIM + SUBLANES, :]
    for jb in range(ROW_TILE // Q_BLOCK):
        wt_ref[0, jb] = wt[:, jb * Q_BLOCK:(jb + 1) * Q_BLOCK]


def _inproj(h, g, wm, bm, wx, bx, cos, sin):
    B, S, D = h.shape
    nt = S // ROW_TILE
    cpt = ROW_TILE // KEY_CHUNK
    qpt = ROW_TILE // Q_BLOCK
    nch = S // KEY_CHUNK
    nq = S // Q_BLOCK
    row = lambda w: pl.BlockSpec((1, ROW_TILE, w), lambda b, t: (b, t, 0))
    full = lambda a: pl.BlockSpec(a.shape, lambda b, t: (0,) * a.ndim)
    tab = pl.BlockSpec((ROW_TILE, 128), lambda b, t: (t, 0))
    qblk = lambda r, w: pl.BlockSpec((1, qpt, r, w), lambda b, t: (b, t, 0, 0))
    out_shape = (
        jax.ShapeDtypeStruct((B, S, 768), F32),
        jax.ShapeDtypeStruct((B, S, 512), F32),
        jax.ShapeDtypeStruct((B, nq, ATTN_W, Q_BLOCK), BF16),
        jax.ShapeDtypeStruct((B, S, ATTN_W), BF16),
        jax.ShapeDtypeStruct((B, nch, ATTN_W, KEY_CHUNK), BF16),
        jax.ShapeDtypeStruct((B, S, ATTN_W), F32),
        jax.ShapeDtypeStruct((B, nq, 128, IDX_HEADS * Q_BLOCK), BF16),
        jax.ShapeDtypeStruct((B, S, 128), BF16),
        jax.ShapeDtypeStruct((B, nq, SUBLANES, Q_BLOCK), F32),
    )
    out_specs = (
        row(768), row(512), qblk(ATTN_W, Q_BLOCK), row(ATTN_W),
        pl.BlockSpec((1, cpt, ATTN_W, KEY_CHUNK), lambda b, t: (b, t, 0, 0)),
        row(ATTN_W), qblk(128, IDX_HEADS * Q_BLOCK), row(128), qblk(SUBLANES, Q_BLOCK),
    )
    return pl.pallas_call(
        _inproj_kernel,
        grid=(B, nt),
        in_specs=[row(D), full(g), full(wm), full(bm), full(wx), full(bx), tab, tab],
        out_specs=out_specs,
        out_shape=out_shape,
        compiler_params=pltpu.CompilerParams(
            dimension_semantics=("arbitrary", "arbitrary"), vmem_limit_bytes=VMEM_LIMIT),
        name="inproj",
    )(h, g, wm, bm, wx, bx, cos, sin)


def _convpool_kernel(zc_ref, zch_ref, zp_ref, zph_ref, dw_ref, dwb_ref, lng_ref, lnb_ref,
                     pw_ref, pwb_ref, plw_ref, plb_ref, pls_ref, o_ref, ubuf, xbuf):
    t = pl.program_id(1)
    keep = (t > 0).astype(F32)

    zc = zc_ref[0]
    zch = zch_ref[0]
    ubuf[0:HALO] = zch[:, 0:256] * _sigmoid(zch[:, 256:512]) * keep
    ubuf[HALO:HALO + ROW_TILE] = zc[:, 0:256] * _sigmoid(zc[:, 256:512])
    acc = jnp.zeros((ROW_TILE, CONV_W), F32) + dwb_ref[...]
    for j in range(CONV_K):
        acc = acc + ubuf[pl.ds(HALO - (CONV_K - 1) + j, ROW_TILE), :] * dw_ref[j:j + 1, :]
    mu = jnp.mean(acc, axis=-1, keepdims=True)
    xc = acc - mu
    y = xc * lax.rsqrt(jnp.mean(xc * xc, axis=-1, keepdims=True) + EPS) * lng_ref[...] + lnb_ref[...]
    y = _silu(y)
    ya = (_dot(y.astype(BF16), pw_ref[...]) + pwb_ref[...]) * _silu(zc[:, 512:768])

    zp = zp_ref[0]
    x = zp[:, 0:256]
    xbuf[0:HALO] = zph_ref[0][:, 0:256] * keep
    xbuf[HALO:HALO + ROW_TILE] = x
    run = x
    sums = {}
    for d in range(1, POOL_WINDOWS[-1]):
        run = run + xbuf[pl.ds(HALO - d, ROW_TILE), :]
        if d + 1 in POOL_WINDOWS:
            sums[d + 1] = run
    lane = lax.broadcasted_iota(jnp.int32, (ROW_TILE, POOL_W), 1)
    grp = lane // POOL_GW
    wsum = jnp.where(grp == 0, sums[2], jnp.where(grp == 1, sums[4],
                                                  jnp.where(grp == 2, sums[8], sums[16])))
    win = jnp.where(grp == 0, 2, jnp.where(grp == 1, 4, jnp.where(grp == 2, 8, 16)))
    pos = t * ROW_TILE + lax.broadcasted_iota(jnp.int32, (ROW_TILE, POOL_W), 0)
    cnt = jnp.minimum(pos + 1, win).astype(F32)
    d = wsum / cnt - x
    yb = (_dot(d.astype(BF16), plw_ref[...]) + plb_ref[...]) * pls_ref[...] * _silu(zp[:, 256:512])

    o_ref[0] = jnp.concatenate([ya, yb], axis=1)


def _convpool(zc, zp, dw, dwb, lng, lnb, pw, pwb, plw, plb, pls):
    B, S, _ = zc.shape
    nt = S // ROW_TILE
    hpt = ROW_TILE // HALO
    row = lambda w: pl.BlockSpec((1, ROW_TILE, w), lambda b, t: (b, t, 0))
    halo = lambda w: pl.BlockSpec((1, HALO, w), lambda b, t: (b, jnp.maximum(t * hpt - 1, 0), 0))
    full = lambda a: pl.BlockSpec(a.shape, lambda b, t: (0,) * a.ndim)
    return pl.pallas_call(
        _convpool_kernel,
        grid=(B, nt),
        in_specs=[row(768), halo(768), row(512), halo(512), full(dw), full(dwb), full(lng),
                  full(lnb), full(pw), full(pwb), full(plw), full(plb), full(pls)],
        out_specs=row(512),
        out_shape=jax.ShapeDtypeStruct((B, S, 512), F32),
        scratch_shapes=[pltpu.VMEM((HALO + ROW_TILE, CONV_W), F32),
                        pltpu.VMEM((HALO + ROW_TILE, POOL_W), F32)],
        compiler_params=pltpu.CompilerParams(
            dimension_semantics=("arbitrary", "arbitrary"), vmem_limit_bytes=VMEM_LIMIT),
        name="convpool",
    )(zc, zc, zp, zp, dw, dwb, lng, lnb, pw, pwb, plw, plb, pls)


def _fold_rows(x, op, fn=lambda p: p):
    parts = [fn(x[r:r + SUBLANES, :]) for r in range(0, x.shape[0], SUBLANES)]
    while len(parts) > 1:
        parts = [op(parts[i], parts[i + 1]) for i in range(0, len(parts), 2)]
    return parts[0]


def _attn_kernel(qt_ref, qi_ref, wt_ref, ag_ref, ws_ref, k_ref, vt_ref, o_ref,
                 sc_ref, qp_ref, acc_ref, lga_ref, lgb_ref, p_ref, *, topk):
    j = pl.program_id(1)
    nfull = j // 2
    nch = nfull + 1
    Q = Q_BLOCK
    KC = KEY_CHUNK
    qpos = j * Q + lax.broadcasted_iota(jnp.int32, (1, Q), 1)

    qi = qi_ref[0, 0]
    wt = wt_ref[0, 0]
    w_h = [wt[h:h + 1, :] for h in range(IDX_HEADS)]

    def unit_scores(u):
        r0 = pl.multiple_of(u * SCAN_ROWS, SCAN_ROWS)
        s = _dot(ws_ref[0, pl.ds(r0, SCAN_ROWS), :], qi)
        tot = jnp.zeros((SCAN_ROWS, Q), F32)
        for h in range(IDX_HEADS):
            tot = tot + w_h[h] * jnp.maximum(s[:, h * Q:(h + 1) * Q], 0.0)
        return tot

    def score_body(i, carry):
        mx, mn = carry
        for u in (2 * i, 2 * i + 1):
            s = unit_scores(u)
            sc_ref[pl.ds(pl.multiple_of(u * SCAN_ROWS, SCAN_ROWS), SCAN_ROWS), :] = s
            mx = jnp.maximum(mx, _fold_rows(s, jnp.maximum))
            mn = jnp.minimum(mn, _fold_rows(s, jnp.minimum))
        return mx, mn

    nscan = (nch + 1) // 2
    nbias = (nch + BIAS_UNROLL - 1) // BIAS_UNROLL
    init = (jnp.full((SUBLANES, Q), -jnp.inf, F32), jnp.full((SUBLANES, Q), jnp.inf, F32))
    mx, mn = lax.fori_loop(0, nbias, score_body, init)
    rowmax = jnp.max(mx, axis=0, keepdims=True)
    rowmin = jnp.min(mn, axis=0, keepdims=True)

    d0 = pl.multiple_of(nfull * KC, KC)
    kpos = nfull * KC + lax.broadcasted_iota(jnp.int32, (KC, Q), 0)
    sc_ref[pl.ds(d0, KC), :] = jnp.where(kpos <= qpos, sc_ref[pl.ds(d0, KC), :], -jnp.inf)

    def pad_body(c, _):
        sc_ref[pl.ds(pl.multiple_of(c * KC, KC), KC), :] = jnp.full((KC, Q), -jnp.inf, F32)
        return 0

    lax.fori_loop(nch, nbias * BIAS_UNROLL, pad_body, 0)

    ncausal = (qpos + 1).astype(F32)
    kk = jnp.minimum(ncausal, float(topk))

    def count_ge(t):
        tb = jnp.broadcast_to(t, (SUBLANES, Q))

        def body(c, acc):
            sv = sc_ref[pl.ds(pl.multiple_of(c * SCAN_ROWS, SCAN_ROWS), SCAN_ROWS), :]
            return acc + _fold_rows(sv, jnp.add, lambda p: jnp.where(p >= tb, 1.0, 0.0))

        acc = lax.fori_loop(0, nscan, body, jnp.zeros((SUBLANES, Q), F32))
        return jnp.sum(acc, axis=0, keepdims=True)

    def bracket_ends(lo, hi):
        lob = jnp.broadcast_to(lo, (SUBLANES, Q))
        hib = jnp.broadcast_to(hi, (SUBLANES, Q))

        def body(c, carry):
            a, b = carry
            sv = sc_ref[pl.ds(pl.multiple_of(c * SCAN_ROWS, SCAN_ROWS), SCAN_ROWS), :]
            a = jnp.minimum(a, _fold_rows(sv, jnp.minimum, lambda p: jnp.where(p >= lob, p, jnp.inf)))
            b = jnp.maximum(b, _fold_rows(sv, jnp.maximum, lambda p: jnp.where(p < hib, p, -jnp.inf)))
            return a, b

        a, b = lax.fori_loop(0, nscan, body, (jnp.full((SUBLANES, Q), jnp.inf, F32),
                                            jnp.full((SUBLANES, Q), -jnp.inf, F32)))
        return jnp.min(a, axis=0, keepdims=True), jnp.max(b, axis=0, keepdims=True)

    res0 = (ncausal <= kk).astype(F32)
    state0 = dict(lo=rowmin, hi=jnp.full((1, Q), jnp.inf, F32), c_hi=jnp.zeros((1, Q), F32),
                  res=res0, thr=rowmin, need=jnp.full((1, Q), NEED_ALL, F32), pivot=rowmax,
                  use_pivot=jnp.ones((1, Q), F32))

    def bisect_pass(_, st):
        mid = jnp.where(st["use_pivot"] > 0, st["pivot"], 0.5 * (st["lo"] + st["hi"]))
        cnt = count_ge(mid)
        live = st["res"] == 0
        ge = cnt >= kk
        hit = live & (cnt == kk)
        up = live & ge
        dn = live & jnp.logical_not(ge)
        return dict(
            lo=jnp.where(up, mid, st["lo"]),
            hi=jnp.where(dn, mid, st["hi"]), c_hi=jnp.where(dn, cnt, st["c_hi"]),
            res=jnp.where(hit, 1.0, st["res"]), thr=jnp.where(hit, mid, st["thr"]),
            need=st["need"], pivot=st["pivot"], use_pivot=jnp.zeros((1, Q), F32))

    def round_body(carry):
        it, st = carry
        st = lax.fori_loop(0, jnp.where(it == 0, FIRST_PASSES, NEXT_PASSES), bisect_pass, st)
        a, b = bracket_ends(st["lo"], st["hi"])
        live = st["res"] == 0
        tie = live & (a == b)
        st = dict(st)
        last = live & (kk - st["c_hi"] == 1.0)
        st["thr"] = jnp.where(tie, a, jnp.where(last, b, st["thr"]))
        st["need"] = jnp.where(tie | last, kk - st["c_hi"], st["need"])
        st["res"] = jnp.where(tie | last, 1.0, st["res"])
        st["lo"] = jnp.where(live, a, st["lo"])
        st["pivot"] = b
        st["use_pivot"] = jnp.ones((1, Q), F32)
        return it + 1, st

    def round_cond(carry):
        _, st = carry
        return jnp.min(st["res"]) == 0

    _, st = lax.while_loop(round_cond, round_body, (jnp.int32(0), state0))
    thr = st["thr"]
    need0 = st["need"]

    qt = qt_ref[0, 0]
    rowhead = lax.broadcasted_iota(jnp.int32, (256, Q), 0) // HEAD_DIM
    zero = jnp.zeros((), BF16)
    for pr in range(N_HEADS // 2):
        qg = qt[(pr // 2) * 256:(pr // 2 + 1) * 256, :]
        qp_ref[pr] = jnp.concatenate([jnp.where(rowhead == (2 * pr) % 4, qg, zero),
                                      jnp.where(rowhead == (2 * pr + 1) % 4, qg, zero)], axis=1)
    acc_ref[...] = jnp.zeros(acc_ref.shape, F32)
    ri = lax.broadcasted_iota(jnp.int32, (KC, KC), 0)
    ci = lax.broadcasted_iota(jnp.int32, (KC, KC), 1)
    tril = jnp.where(ci <= ri, 1.0, 0.0).astype(BF16)

    def bias_body(u, need):
        for i in range(BIAS_UNROLL):
            r0 = pl.multiple_of((u * BIAS_UNROLL + i) * KC, KC)
            sv = sc_ref[pl.ds(r0, KC), :]
            eq = sv == thr
            prefix = _dot(tril, jnp.where(eq, 1.0, 0.0).astype(BF16))
            take_tie = jnp.where(prefix <= need, 0.0, MASK_BIAS)
            sc_ref[pl.ds(r0, KC), :] = jnp.where(sv > thr, 0.0, jnp.where(eq, take_tie, MASK_BIAS))
            ties = _fold_rows(sv, jnp.add, lambda p: jnp.where(p == thr, 1.0, 0.0))
            need = need - jnp.sum(ties, axis=0, keepdims=True)
        return need

    lax.fori_loop(0, nbias, bias_body, need0)

    def logits_into(c, lg_ref):
        r0 = pl.multiple_of(c * KC, KC)
        bias = sc_ref[pl.ds(r0, KC), :]
        bias2 = jnp.concatenate([bias, bias], axis=1)
        for pr in range(N_HEADS // 2):
            g0 = (pr // 2) * 256
            lg_ref[pr] = _dot(k_ref[0, pl.ds(r0, KC), g0:g0 + 256], qp_ref[pr]) + bias2

    def softmax_step(c, lg_ref, ms, ls):
        new_ms, new_ls = [], []
        for pr in range(N_HEADS // 2):
            m_new = jnp.maximum(ms[pr], jnp.max(_fold_rows(lg_ref[pr], jnp.maximum), axis=0,
                                                keepdims=True))
            alpha = jnp.exp2(ms[pr] - m_new)
            sums = []
            for r in range(0, KC, 2 * SUBLANES):
                pe = jnp.exp2(lg_ref[pr, r:r + 2 * SUBLANES, :] - m_new)
                p_ref[pr, r:r + 2 * SUBLANES, :] = pe.astype(BF16)
                sums.append(pe[0:SUBLANES, :] + pe[SUBLANES:2 * SUBLANES, :])
            while len(sums) > 1:
                sums = [sums[i] + sums[i + 1] for i in range(0, len(sums), 2)]
            new_ls.append(alpha * ls[pr] + jnp.sum(sums[0], axis=0, keepdims=True))
            new_ms.append(m_new)
            o = _dot(vt_ref[0, c, pr * 128:(pr + 1) * 128, :], p_ref[pr])
            a0 = pr * 128
            acc_ref[a0:a0 + 64, :] = acc_ref[a0:a0 + 64, :] * alpha[:, 0:Q] + o[0:64, 0:Q]
            acc_ref[a0 + 64:a0 + 128, :] = (acc_ref[a0 + 64:a0 + 128, :] * alpha[:, Q:2 * Q]
                                            + o[64:128, Q:2 * Q])
        return tuple(new_ms), tuple(new_ls)

    logits_into(0, lga_ref)

    def attn_body(i, carry):
        ms, ls = carry
        c = 2 * i
        logits_into(c + 1, lgb_ref)
        ms, ls = softmax_step(c, lga_ref, ms, ls)
        logits_into(jnp.minimum(c + 2, 2 * nscan - 1), lga_ref)
        return softmax_step(c + 1, lgb_ref, ms, ls)

    ms0 = tuple(jnp.full((1, 2 * Q), MASK_BIAS, F32) for _ in range(N_HEADS // 2))
    ls0 = tuple(jnp.zeros((1, 2 * Q), F32) for _ in range(N_HEADS // 2))
    _, ls = lax.fori_loop(0, nscan, attn_body, (ms0, ls0))
    outs = []
    for pr in range(N_HEADS // 2):
        a0 = pr * 128
        outs.append(acc_ref[a0:a0 + 64, :] / ls[pr][:, 0:Q])
        outs.append(acc_ref[a0 + 64:a0 + 128, :] / ls[pr][:, Q:2 * Q])
    o_ref[0] = jnp.concatenate(outs, axis=0).T * _silu(ag_ref[0])


def _attention(qt, qi, wt, ag, ws, k, vt, topk):
    B, S, _ = k.shape
    nq = S // Q_BLOCK
    nch = S // KEY_CHUNK
    qblk = lambda r, w: pl.BlockSpec((1, 1, r, w), lambda b, j: (b, j, 0, 0))
    once = pl.Buffered(1)
    seq = lambda w: pl.BlockSpec((1, S, w), lambda b, j: (b, 0, 0), pipeline_mode=once)
    return pl.pallas_call(
        functools.partial(_attn_kernel, topk=topk),
        grid=(B, nq),
        in_specs=[
            qblk(ATTN_W, Q_BLOCK), qblk(128, IDX_HEADS * Q_BLOCK), qblk(SUBLANES, Q_BLOCK),
            pl.BlockSpec((1, Q_BLOCK, ATTN_W), lambda b, j: (b, j, 0)),
            seq(128), seq(ATTN_W),
            pl.BlockSpec((1, nch, ATTN_W, KEY_CHUNK), lambda b, j: (b, 0, 0, 0), pipeline_mode=once),
        ],
        out_specs=pl.BlockSpec((1, Q_BLOCK, ATTN_W), lambda b, j: (b, j, 0)),
        out_shape=jax.ShapeDtypeStruct((B, S, ATTN_W), F32),
        scratch_shapes=[
            pltpu.VMEM((S, Q_BLOCK), F32),
            pltpu.VMEM((N_HEADS // 2, 256, 2 * Q_BLOCK), BF16),
            pltpu.VMEM((ATTN_W, Q_BLOCK), F32),
            pltpu.VMEM((N_HEADS // 2, KEY_CHUNK, 2 * Q_BLOCK), F32),
            pltpu.VMEM((N_HEADS // 2, KEY_CHUNK, 2 * Q_BLOCK), F32),
            pltpu.VMEM((N_HEADS // 2, KEY_CHUNK, 2 * Q_BLOCK), BF16),
        ],
        compiler_params=pltpu.CompilerParams(
            dimension_semantics=("arbitrary", "arbitrary"), vmem_limit_bytes=VMEM_LIMIT),
        name="attention",
    )(qt, qi, wt, ag, ws, k, vt)


def _outproj_kernel(h_ref, yab_ref, yc_ref, p_ref, wo_ref, plew_ref, pleg_ref, fng_ref, o_ref,
                    *, final):
    h = (h_ref[0] + _dot(yab_ref[0].astype(BF16), wo_ref[0:512, :])
         + _dot(yc_ref[0].astype(BF16), wo_ref[512:1024, :]))
    ple = _dot(p_ref[0].astype(BF16), plew_ref[...])
    h = h + ple * _sigmoid(_dot(h.astype(BF16), pleg_ref[...]))
    if final:
        h = h * lax.rsqrt(jnp.mean(h * h, axis=-1, keepdims=True) + EPS) * fng_ref[...]
    o_ref[0] = h


def _outproj(h, yab, yc, p, wo, plew, pleg, fng, final):
    B, S, D = h.shape
    nt = S // ROW_TILE
    row = lambda w: pl.BlockSpec((1, ROW_TILE, w), lambda b, t: (b, t, 0))
    full = lambda a: pl.BlockSpec(a.shape, lambda b, t: (0,) * a.ndim)
    return pl.pallas_call(
        functools.partial(_outproj_kernel, final=final),
        grid=(B, nt),
        in_specs=[row(D), row(512), row(512), row(p.shape[-1]), full(wo), full(plew), full(pleg),
                  full(fng)],
        out_specs=row(D),
        out_shape=jax.ShapeDtypeStruct((B, S, D), F32),
        compiler_params=pltpu.CompilerParams(
            dimension_semantics=("arbitrary", "arbitrary"), vmem_limit_bytes=VMEM_LIMIT),
        name="outproj",
    )(h, yab, yc, p, wo, plew, pleg, fng)


def _block_diag(w):
    g, a, b = w.shape
    out = jnp.zeros((g * a, g * b), w.dtype)
    for i in range(g):
        out = out.at[i * a:(i + 1) * a, i * b:(i + 1) * b].set(w[i])
    return out


def kernel(x, p, norm_g, w_in, b_in, conv_dw_w, conv_dw_b, conv_ln_g, conv_ln_b, conv_pw_w,
           conv_pw_b, pool_w, pool_b, pool_scale, w_out, ple_w, ple_gate_w, final_norm_g):
    B, S, D = x.shape
    depth = w_in.shape[0]
    assert S % ROW_TILE == 0 and D == 1024 and w_in.shape[2] == N_MAIN + 324
    topk = min(TOPK_MAX, S // 4)

    half = HEAD_DIM // 2
    inv = ROPE_THETA ** (-jnp.arange(half, dtype=F32) / half)
    ang = jnp.arange(S).astype(F32)[:, None] * inv[None, :]
    cos = jnp.tile(jnp.cos(ang), (1, 4))
    sin = jnp.tile(jnp.concatenate([-jnp.sin(ang), jnp.sin(ang)], axis=1), (1, 2))

    row2 = lambda a: a.reshape(1, -1)
    h = x
    for i in range(depth):
        wm = w_in[i, :, :N_MAIN].astype(BF16)
        bm = row2(b_in[i, :N_MAIN])
        wx = jnp.pad(w_in[i, :, N_MAIN:], ((0, 0), (0, IDX_PAD - 324)))
        bx = row2(jnp.pad(b_in[i, N_MAIN:], (0, IDX_PAD - 324)))
        zc, zp, qt, k, vt, ag, qi, ws, wt = _inproj(
            h, row2(norm_g[i]), wm, bm, wx, bx, cos, sin)
        yab = _convpool(zc, zp, conv_dw_w[i], row2(conv_dw_b[i]), row2(conv_ln_g[i]),
                        row2(conv_ln_b[i]), conv_pw_w[i].astype(BF16), row2(conv_pw_b[i]),
                        _block_diag(pool_w[i]).astype(BF16), row2(pool_b[i]), row2(pool_scale[i]))
        yc = _attention(qt, qi, wt, ag, ws, k, vt, topk)
        h = _outproj(h, yab, yc, p[i], w_out[i].astype(BF16), ple_w[i].astype(BF16),
                     ple_gate_w[i].astype(BF16), row2(final_norm_g), final=(i == depth - 1))
    return h
```

```python
import functools

import jax
import jax.numpy as jnp
from jax import lax
from jax.experimental import pallas as pl
from jax.experimental.pallas import tpu as pltpu

F32 = jnp.float32
BF16 = jnp.bfloat16

CONV_W = 256
CONV_K = 31
POOL_W = 256
POOL_GW = 64
POOL_WINDOWS = (2, 4, 8, 16)
HEAD_DIM = 64
ATTN_W = 512
N_HEADS = 8
IDX_HEADS = 4
IDX_DIM = 64
TOPK_MAX = 256
ROPE_THETA = 10000.0
EPS = 1e-6

SUBLANES = 8
Q_BLOCK = 128
KEY_CHUNK = 256
SCAN_ROWS = 2 * KEY_CHUNK
BIAS_UNROLL = 4
PAIR_ROWS = 128 + 16
VT_ROWS = (N_HEADS // 2) * PAIR_ROWS
FIRST_PASSES = 16
NEXT_PASSES = 2
INTERP_AFTER = 5
INTERP_CLIP = 0.25
LOG2E = 1.4426950408889634
HALO = 32
ROW_TILE = 512
IDX_PAD = 384
N_MAIN = 3 * CONV_W + 2 * POOL_W + 4 * ATTN_W
MASK_BIAS = -1e30
NEED_ALL = 1e9
VMEM_LIMIT = 60 * 1024 * 1024
HIGHEST = lax.Precision.HIGHEST


def _sigmoid(x):
    return 1.0 / (1.0 + jnp.exp(-x))


def _silu(x):
    return x * _sigmoid(x)


def _dot(a, b, precision=None):
    return lax.dot_general(a, b, (((1,), (0,)), ((), ())), precision=precision,
                           preferred_element_type=F32)


def _rope_slab(x, cos, sin):
    lane = lax.broadcasted_iota(jnp.int32, x.shape, 1)
    swapped = jnp.where((lane & 32) == 0, pltpu.roll(x, 96, 1), pltpu.roll(x, 32, 1))
    return x * cos + swapped * sin


def _rope(x, cos, sin):
    return jnp.concatenate(
        [_rope_slab(x[:, s:s + 128], cos, sin) for s in range(0, x.shape[1], 128)], axis=1)


def _inproj_kernel(h_ref, g_ref, wm_ref, bm_ref, wx_ref, bx_ref, cos_ref, sin_ref,
                   zc_ref, zp_ref, qt_ref, k_ref, vt_ref, ag_ref, qi_ref, ws_ref, wt_ref):
    x = h_ref[0]
    n = x * lax.rsqrt(jnp.mean(x * x, axis=-1, keepdims=True) + EPS) * g_ref[...]
    nb = n.astype(BF16)
    cos = cos_ref[...]
    sin = sin_ref[...]

    def proj(a, b):
        return _dot(nb, wm_ref[:, a:b]) + bm_ref[:, a:b]

    zc_ref[0] = proj(0, 768)
    zp_ref[0] = proj(768, 1280)
    qt = (_rope(proj(1280, 1792), cos, sin) * (HEAD_DIM ** -0.5 * LOG2E)).T.astype(BF16)
    for jb in range(ROW_TILE // Q_BLOCK):
        qt_ref[0, jb] = qt[:, jb * Q_BLOCK:(jb + 1) * Q_BLOCK]
    k_ref[0] = _rope(proj(1792, 2304), cos, sin).astype(BF16)
    vt = proj(2304, 2816).T.astype(BF16)
    ones = jnp.ones((PAIR_ROWS - 128, KEY_CHUNK), BF16)
    for c in range(ROW_TILE // KEY_CHUNK):
        vt_ref[0, c] = jnp.concatenate(
            [piece for pr in range(N_HEADS // 2)
             for piece in (vt[pr * 128:(pr + 1) * 128, c * KEY_CHUNK:(c + 1) * KEY_CHUNK], ones)], axis=0)
    ag_ref[0] = proj(2816, 3328)

    zx = _dot(n, wx_ref[...], precision=HIGHEST) + bx_ref[...]
    qit = (_rope(zx[:, 0:256], cos, sin) * (IDX_DIM ** -0.5)).T.astype(BF16)
    pad = jnp.zeros((128 - IDX_DIM, Q_BLOCK), BF16)
    for jb in range(ROW_TILE // Q_BLOCK):
        qi_ref[0, jb] = jnp.concatenate(
            [jnp.concatenate([qit[h * IDX_DIM:(h + 1) * IDX_DIM, jb * Q_BLOCK:(jb + 1) * Q_BLOCK],
                              pad], axis=0) for h in range(IDX_HEADS)], axis=1)
    tail = zx[:, 256:384]
    lane = lax.broadcasted_iota(jnp.int32, tail.shape, 1)
    tail = jnp.where(lane < IDX_DIM, _rope_slab(tail, cos, sin), tail)
    ws_ref[0] = tail.astype(BF16)
    wt = tail.T[IDX_DIM:IDX_DIM + SUBLANES, :]
    for jb in range(ROW_TILE // Q_BLOCK):
        wt_ref[0, jb] = wt[:, jb * Q_BLOCK:(jb + 1) * Q_BLOCK]


def _inproj(h, g, wm, bm, wx, bx, cos, sin):
    B, S, D = h.shape
    nt = S // ROW_TILE
    cpt = ROW_TILE // KEY_CHUNK
    qpt = ROW_TILE // Q_BLOCK
    nch = S // KEY_CHUNK
    nq = S // Q_BLOCK
    row = lambda w: pl.BlockSpec((1, ROW_TILE, w), lambda b, t: (b, t, 0))
    full = lambda a: pl.BlockSpec(a.shape, lambda b, t: (0,) * a.ndim)
    tab = pl.BlockSpec((ROW_TILE, 128), lambda b, t: (t, 0))
    qblk = lambda r, w: pl.BlockSpec((1, qpt, r, w), lambda b, t: (b, t, 0, 0))
    out_shape = (
        jax.ShapeDtypeStruct((B, S, 768), F32),
        jax.ShapeDtypeStruct((B, S, 512), F32),
        jax.ShapeDtypeStruct((B, nq, ATTN_W, Q_BLOCK), BF16),
        jax.ShapeDtypeStruct((B, S, ATTN_W), BF16),
        jax.ShapeDtypeStruct((B, nch, VT_ROWS, KEY_CHUNK), BF16),
        jax.ShapeDtypeStruct((B, S, ATTN_W), F32),
        jax.ShapeDtypeStruct((B, nq, 128, IDX_HEADS * Q_BLOCK), BF16),
        jax.ShapeDtypeStruct((B, S, 128), BF16),
        jax.ShapeDtypeStruct((B, nq, SUBLANES, Q_BLOCK), F32),
    )
    out_specs = (
        row(768), row(512), qblk(ATTN_W, Q_BLOCK), row(ATTN_W),
        pl.BlockSpec((1, cpt, VT_ROWS, KEY_CHUNK), lambda b, t: (b, t, 0, 0)),
        row(ATTN_W), qblk(128, IDX_HEADS * Q_BLOCK), row(128), qblk(SUBLANES, Q_BLOCK),
    )
    return pl.pallas_call(
        _inproj_kernel,
        grid=(B, nt),
        in_specs=[row(D), full(g), full(wm), full(bm), full(wx), full(bx), tab, tab],
        out_specs=out_specs,
        out_shape=out_shape,
        compiler_params=pltpu.CompilerParams(
            dimension_semantics=("arbitrary", "arbitrary"), vmem_limit_bytes=VMEM_LIMIT),
        name="inproj",
    )(h, g, wm, bm, wx, bx, cos, sin)


def _convpool_kernel(zc_ref, zch_ref, zp_ref, zph_ref, dw_ref, dwb_ref, lng_ref, lnb_ref,
                     pw_ref, pwb_ref, plw_ref, plb_ref, pls_ref, o_ref, ubuf, xbuf):
    t = pl.program_id(1)
    keep = (t > 0).astype(F32)

    zc = zc_ref[0]
    zch = zch_ref[0]
    ubuf[0:HALO] = zch[:, 0:256] * _sigmoid(zch[:, 256:512]) * keep
    ubuf[HALO:HALO + ROW_TILE] = zc[:, 0:256] * _sigmoid(zc[:, 256:512])
    acc = jnp.zeros((ROW_TILE, CONV_W), F32) + dwb_ref[...]
    for j in range(CONV_K):
        acc = acc + ubuf[pl.ds(HALO - (CONV_K - 1) + j, ROW_TILE), :] * dw_ref[j:j + 1, :]
    mu = jnp.mean(acc, axis=-1, keepdims=True)
    xc = acc - mu
    y = xc * lax.rsqrt(jnp.mean(xc * xc, axis=-1, keepdims=True) + EPS) * lng_ref[...] + lnb_ref[...]
    y = _silu(y)
    ya = (_dot(y.astype(BF16), pw_ref[...]) + pwb_ref[...]) * _silu(zc[:, 512:768])

    zp = zp_ref[0]
    x = zp[:, 0:256]
    xbuf[0:HALO] = zph_ref[0][:, 0:256] * keep
    xbuf[HALO:HALO + ROW_TILE] = x
    run = x
    sums = {}
    for d in range(1, POOL_WINDOWS[-1]):
        run = run + xbuf[pl.ds(HALO - d, ROW_TILE), :]
        if d + 1 in POOL_WINDOWS:
            sums[d + 1] = run
    lane = lax.broadcasted_iota(jnp.int32, (ROW_TILE, POOL_W), 1)
    grp = lane // POOL_GW
    wsum = jnp.where(grp == 0, sums[2], jnp.where(grp == 1, sums[4],
                                                  jnp.where(grp == 2, sums[8], sums[16])))
    win = jnp.where(grp == 0, 2, jnp.where(grp == 1, 4, jnp.where(grp == 2, 8, 16)))
    pos = t * ROW_TILE + lax.broadcasted_iota(jnp.int32, (ROW_TILE, POOL_W), 0)
    cnt = jnp.minimum(pos + 1, win).astype(F32)
    d = wsum / cnt - x
    yb = (_dot(d.astype(BF16), plw_ref[...]) + plb_ref[...]) * pls_ref[...] * _silu(zp[:, 256:512])

    o_ref[0] = jnp.concatenate([ya, yb], axis=1)


def _convpool(zc, zp, dw, dwb, lng, lnb, pw, pwb, plw, plb, pls):
    B, S, _ = zc.shape
    nt = S // ROW_TILE
    hpt = ROW_TILE // HALO
    row = lambda w: pl.BlockSpec((1, ROW_TILE, w), lambda b, t: (b, t, 0))
    halo = lambda w: pl.BlockSpec((1, HALO, w), lambda b, t: (b, jnp.maximum(t * hpt - 1, 0), 0))
    full = lambda a: pl.BlockSpec(a.shape, lambda b, t: (0,) * a.ndim)
    return pl.pallas_call(
        _convpool_kernel,
        grid=(B, nt),
        in_specs=[row(768), halo(768), row(512), halo(512), full(dw), full(dwb), full(lng),
                  full(lnb), full(pw), full(pwb), full(plw), full(plb), full(pls)],
        out_specs=row(512),
        out_shape=jax.ShapeDtypeStruct((B, S, 512), F32),
        scratch_shapes=[pltpu.VMEM((HALO + ROW_TILE, CONV_W), F32),
                        pltpu.VMEM((HALO + ROW_TILE, POOL_W), F32)],
        compiler_params=pltpu.CompilerParams(
            dimension_semantics=("arbitrary", "arbitrary"), vmem_limit_bytes=VMEM_LIMIT),
        name="convpool",
    )(zc, zc, zp, zp, dw, dwb, lng, lnb, pw, pwb, plw, plb, pls)


def _fold_rows(x, op, fn=lambda p: p):
    return _combine([fn(x[r:r + SUBLANES, :]) for r in range(0, x.shape[0], SUBLANES)], op)


def _combine(parts, op):
    while len(parts) > 1:
        parts = [op(parts[i], parts[i + 1]) for i in range(0, len(parts) - 1, 2)] + (
            [parts[-1]] if len(parts) % 2 else [])
    return parts[0]


def _attn_kernel(qt_ref, qi_ref, wt_ref, ag_ref, ws_ref, k_ref, vt_ref, o_ref,
                 sc_ref, qp_ref, acc_ref, lga_ref, lgb_ref, p_ref, *, topk):
    j = pl.program_id(1)
    nfull = j // 2
    nch = nfull + 1
    Q = Q_BLOCK
    KC = KEY_CHUNK
    qpos = j * Q + lax.broadcasted_iota(jnp.int32, (1, Q), 1)

    qi = qi_ref[0, 0]
    wt = wt_ref[0, 0]
    w_h = [wt[h:h + 1, :] for h in range(IDX_HEADS)]

    def unit_scores(u):
        r0 = pl.multiple_of(u * SCAN_ROWS, SCAN_ROWS)
        s = _dot(ws_ref[0, pl.ds(r0, SCAN_ROWS), :], qi)
        tot = jnp.zeros((SCAN_ROWS, Q), F32)
        for h in range(IDX_HEADS):
            tot = tot + w_h[h] * jnp.maximum(s[:, h * Q:(h + 1) * Q], 0.0)
        return tot

    def score_body(i, carry):
        mx, mn = carry
        for u in (2 * i, 2 * i + 1):
            s = unit_scores(u)
            sc_ref[pl.ds(pl.multiple_of(u * SCAN_ROWS, SCAN_ROWS), SCAN_ROWS), :] = s
            mx = jnp.maximum(mx, _fold_rows(s, jnp.maximum))
            mn = jnp.minimum(mn, _fold_rows(s, jnp.minimum))
        return mx, mn

    nscan = (nch + 1) // 2
    nbias = (nch + BIAS_UNROLL - 1) // BIAS_UNROLL
    init = (jnp.full((SUBLANES, Q), -jnp.inf, F32), jnp.full((SUBLANES, Q), jnp.inf, F32))
    mx, mn = lax.fori_loop(0, nbias, score_body, init)
    rowmax = jnp.max(mx, axis=0, keepdims=True)
    rowmin = jnp.min(mn, axis=0, keepdims=True)

    d0 = pl.multiple_of(nfull * KC, KC)
    kpos = nfull * KC + lax.broadcasted_iota(jnp.int32, (KC, Q), 0)
    sc_ref[pl.ds(d0, KC), :] = jnp.where(kpos <= qpos, sc_ref[pl.ds(d0, KC), :], -jnp.inf)

    def pad_body(c, _):
        sc_ref[pl.ds(pl.multiple_of(c * KC, KC), KC), :] = jnp.full((KC, Q), -jnp.inf, F32)
        return 0

    lax.fori_loop(nch, nbias * BIAS_UNROLL, pad_body, 0)

    ncausal = (qpos + 1).astype(F32)
    kk = jnp.minimum(ncausal, float(topk))

    def count_ge(t):
        tb = jnp.broadcast_to(t, (SUBLANES, Q))

        def body(c, acc):
            sv = sc_ref[pl.ds(pl.multiple_of(c * SCAN_ROWS, SCAN_ROWS), SCAN_ROWS), :]
            return acc + _fold_rows(sv, jnp.add, lambda p: jnp.where(p >= tb, 1.0, 0.0))

        acc = lax.fori_loop(0, nscan, body, jnp.zeros((SUBLANES, Q), F32))
        return jnp.sum(acc, axis=0, keepdims=True)

    def bracket_ends(lo, hi):
        lob = jnp.broadcast_to(lo, (SUBLANES, Q))
        hib = jnp.broadcast_to(hi, (SUBLANES, Q))

        def body(c, carry):
            a, b = carry
            sv = sc_ref[pl.ds(pl.multiple_of(c * SCAN_ROWS, SCAN_ROWS), SCAN_ROWS), :]
            a = jnp.minimum(a, _fold_rows(sv, jnp.minimum, lambda p: jnp.where(p >= lob, p, jnp.inf)))
            b = jnp.maximum(b, _fold_rows(sv, jnp.maximum, lambda p: jnp.where(p < hib, p, -jnp.inf)))
            return a, b

        a, b = lax.fori_loop(0, nscan, body, (jnp.full((SUBLANES, Q), jnp.inf, F32),
                                            jnp.full((SUBLANES, Q), -jnp.inf, F32)))
        return jnp.min(a, axis=0, keepdims=True), jnp.max(b, axis=0, keepdims=True)

    res0 = (ncausal <= kk).astype(F32)
    state0 = dict(lo=rowmin, hi=jnp.full((1, Q), jnp.inf, F32), c_lo=ncausal,
                  c_hi=jnp.zeros((1, Q), F32), res=res0, thr=rowmin,
                  need=jnp.full((1, Q), NEED_ALL, F32), pivot=rowmax,
                  use_pivot=jnp.ones((1, Q), F32), npass=jnp.zeros((1, Q), F32))

    def bisect_pass(_, st):
        lo, hi = st["lo"], st["hi"]
        bounded = hi < jnp.inf
        frac = jnp.clip((st["c_lo"] - kk + 0.5) / jnp.maximum(st["c_lo"] - st["c_hi"], 1.0),
                        INTERP_CLIP, 1.0 - INTERP_CLIP)
        frac = jnp.where(st["npass"] >= INTERP_AFTER, frac, 0.5)
        mid = jnp.where(st["use_pivot"] > 0, st["pivot"],
                        jnp.where(bounded, lo + (hi - lo) * frac, lo))
        cnt = count_ge(mid)
        live = st["res"] == 0
        ge = cnt >= kk
        hit = live & (cnt == kk)
        up = live & ge
        dn = live & jnp.logical_not(ge)
        return dict(
            lo=jnp.where(up, mid, lo), c_lo=jnp.where(up, cnt, st["c_lo"]),
            hi=jnp.where(dn, mid, hi), c_hi=jnp.where(dn, cnt, st["c_hi"]),
            res=jnp.where(hit, 1.0, st["res"]), thr=jnp.where(hit, mid, st["thr"]),
            need=st["need"], pivot=st["pivot"], use_pivot=jnp.zeros((1, Q), F32),
            npass=st["npass"] + 1.0)

    def round_body(carry):
        it, st = carry
        st = lax.fori_loop(0, jnp.where(it == 0, FIRST_PASSES, NEXT_PASSES), bisect_pass, st)
        a, b = bracket_ends(st["lo"], st["hi"])
        live = st["res"] == 0
        tie = live & (a == b)
        st = dict(st)
        last = live & (kk - st["c_hi"] == 1.0)
        st["thr"] = jnp.where(tie, a, jnp.where(last, b, st["thr"]))
        st["need"] = jnp.where(tie | last, kk - st["c_hi"], st["need"])
        st["res"] = jnp.where(tie | last, 1.0, st["res"])
        st["lo"] = jnp.where(live, a, st["lo"])
        st["pivot"] = b
        st["use_pivot"] = jnp.ones((1, Q), F32)
        return it + 1, st

    def round_cond(carry):
        _, st = carry
        return jnp.min(st["res"]) == 0

    _, st = lax.while_loop(round_cond, round_body, (jnp.int32(0), state0))
    thr = st["thr"]
    need0 = st["need"]

    qt = qt_ref[0, 0]
    rowhead = lax.broadcasted_iota(jnp.int32, (256, Q), 0) // HEAD_DIM
    zero = jnp.zeros((), BF16)
    for pr in range(N_HEADS // 2):
        qg = qt[(pr // 2) * 256:(pr // 2 + 1) * 256, :]
        qp_ref[pr] = jnp.concatenate([jnp.where(rowhead == (2 * pr) % 4, qg, zero),
                                      jnp.where(rowhead == (2 * pr + 1) % 4, qg, zero)], axis=1)
    acc_ref[...] = jnp.zeros(acc_ref.shape, F32)
    ri = lax.broadcasted_iota(jnp.int32, (KC, KC), 0)
    ci = lax.broadcasted_iota(jnp.int32, (KC, KC), 1)
    tril = jnp.where(ci <= ri, 1.0, 0.0).astype(BF16)

    def bias_body(u, need):
        for i in range(BIAS_UNROLL):
            r0 = pl.multiple_of((u * BIAS_UNROLL + i) * KC, KC)
            sv = sc_ref[pl.ds(r0, KC), :]
            eq = sv == thr
            prefix = _dot(tril, jnp.where(eq, 1.0, 0.0).astype(BF16))
            take_tie = jnp.where(prefix <= need, 0.0, MASK_BIAS)
            sc_ref[pl.ds(r0, KC), :] = jnp.where(sv > thr, 0.0, jnp.where(eq, take_tie, MASK_BIAS))
            ties = _fold_rows(sv, jnp.add, lambda p: jnp.where(p == thr, 1.0, 0.0))
            need = need - jnp.sum(ties, axis=0, keepdims=True)
        return need

    lax.fori_loop(0, nbias, bias_body, need0)

    def logits_into(c, lg_ref):
        r0 = pl.multiple_of(c * KC, KC)
        bias = sc_ref[pl.ds(r0, KC), :]
        bias2 = jnp.concatenate([bias, bias], axis=1)
        for pr in range(N_HEADS // 2):
            g0 = (pr // 2) * 256
            lg_ref[pr] = _dot(k_ref[0, pl.ds(r0, KC), g0:g0 + 256], qp_ref[pr]) + bias2

    def softmax_step(c, lg_ref, ms, ls):
        new_ms, new_ls = [], []
        for pr in range(N_HEADS // 2):
            m_new = jnp.maximum(ms[pr], jnp.max(_fold_rows(lg_ref[pr], jnp.maximum), axis=0,
                                                keepdims=True))
            alpha = jnp.exp2(ms[pr] - m_new)
            for r in range(0, KC, 2 * SUBLANES):
                pe = jnp.exp2(lg_ref[pr, r:r + 2 * SUBLANES, :] - m_new)
                p_ref[pr, r:r + 2 * SUBLANES, :] = pe.astype(BF16)
            o = _dot(vt_ref[0, c, pr * PAIR_ROWS:(pr + 1) * PAIR_ROWS, :], p_ref[pr])
            new_ls.append(alpha * ls[pr] + o[128:129, :])
            new_ms.append(m_new)
            a0 = pr * 128
            acc_ref[a0:a0 + 64, :] = acc_ref[a0:a0 + 64, :] * alpha[:, 0:Q] + o[0:64, 0:Q]
            acc_ref[a0 + 64:a0 + 128, :] = (acc_ref[a0 + 64:a0 + 128, :] * alpha[:, Q:2 * Q]
                                            + o[64:128, Q:2 * Q])
        return tuple(new_ms), tuple(new_ls)

    logits_into(0, lga_ref)

    def attn_body(i, carry):
        ms, ls = carry
        c = 2 * i
        logits_into(c + 1, lgb_ref)
        ms, ls = softmax_step(c, lga_ref, ms, ls)
        logits_into(jnp.minimum(c + 2, 2 * nscan - 1), lga_ref)
        return softmax_step(c + 1, lgb_ref, ms, ls)

    ms0 = tuple(jnp.full((1, 2 * Q), MASK_BIAS, F32) for _ in range(N_HEADS // 2))
    ls0 = tuple(jnp.zeros((1, 2 * Q), F32) for _ in range(N_HEADS // 2))
    _, ls = lax.fori_loop(0, nscan, attn_body, (ms0, ls0))
    outs = []
    for pr in range(N_HEADS // 2):
        a0 = pr * 128
        outs.append(acc_ref[a0:a0 + 64, :] / ls[pr][:, 0:Q])
        outs.append(acc_ref[a0 + 64:a0 + 128, :] / ls[pr][:, Q:2 * Q])
    o_ref[0] = jnp.concatenate(outs, axis=0).T * _silu(ag_ref[0])


def _attention(qt, qi, wt, ag, ws, k, vt, topk):
    B, S, _ = k.shape
    nq = S // Q_BLOCK
    nch = S // KEY_CHUNK
    qblk = lambda r, w: pl.BlockSpec((1, 1, r, w), lambda b, j: (b, j, 0, 0))
    once = pl.Buffered(1)
    seq = lambda w: pl.BlockSpec((1, S, w), lambda b, j: (b, 0, 0), pipeline_mode=once)
    return pl.pallas_call(
        functools.partial(_attn_kernel, topk=topk),
        grid=(B, nq),
        in_specs=[
            qblk(ATTN_W, Q_BLOCK), qblk(128, IDX_HEADS * Q_BLOCK), qblk(SUBLANES, Q_BLOCK),
            pl.BlockSpec((1, Q_BLOCK, ATTN_W), lambda b, j: (b, j, 0)),
            seq(128), seq(ATTN_W),
            pl.BlockSpec((1, nch, VT_ROWS, KEY_CHUNK), lambda b, j: (b, 0, 0, 0), pipeline_mode=once),
        ],
        out_specs=pl.BlockSpec((1, Q_BLOCK, ATTN_W), lambda b, j: (b, j, 0)),
        out_shape=jax.ShapeDtypeStruct((B, S, ATTN_W), F32),
        scratch_shapes=[
            pltpu.VMEM((S, Q_BLOCK), F32),
            pltpu.VMEM((N_HEADS // 2, 256, 2 * Q_BLOCK), BF16),
            pltpu.VMEM((ATTN_W, Q_BLOCK), F32),
            pltpu.VMEM((N_HEADS // 2, KEY_CHUNK, 2 * Q_BLOCK), F32),
            pltpu.VMEM((N_HEADS // 2, KEY_CHUNK, 2 * Q_BLOCK), F32),
            pltpu.VMEM((N_HEADS // 2, KEY_CHUNK, 2 * Q_BLOCK), BF16),
        ],
        compiler_params=pltpu.CompilerParams(
            dimension_semantics=("arbitrary", "arbitrary"), vmem_limit_bytes=VMEM_LIMIT),
        name="attention",
    )(qt, qi, wt, ag, ws, k, vt)


def _outproj_kernel(h_ref, yab_ref, yc_ref, p_ref, wo_ref, plew_ref, pleg_ref, fng_ref, o_ref,
                    *, final):
    h = (h_ref[0] + _dot(yab_ref[0].astype(BF16), wo_ref[0:512, :])
         + _dot(yc_ref[0].astype(BF16), wo_ref[512:1024, :]))
    ple = _dot(p_ref[0].astype(BF16), plew_ref[...])
    h = h + ple * _sigmoid(_dot(h.astype(BF16), pleg_ref[...]))
    if final:
        h = h * lax.rsqrt(jnp.mean(h * h, axis=-1, keepdims=True) + EPS) * fng_ref[...]
    o_ref[0] = h


def _outproj(h, yab, yc, p, wo, plew, pleg, fng, final):
    B, S, D = h.shape
    nt = S // ROW_TILE
    row = lambda w: pl.BlockSpec((1, ROW_TILE, w), lambda b, t: (b, t, 0))
    full = lambda a: pl.BlockSpec(a.shape, lambda b, t: (0,) * a.ndim)
    return pl.pallas_call(
        functools.partial(_outproj_kernel, final=final),
        grid=(B, nt),
        in_specs=[row(D), row(512), row(512), row(p.shape[-1]), full(wo), full(plew), full(pleg),
                  full(fng)],
        out_specs=row(D),
        out_shape=jax.ShapeDtypeStruct((B, S, D), F32),
        compiler_params=pltpu.CompilerParams(
            dimension_semantics=("arbitrary", "arbitrary"), vmem_limit_bytes=VMEM_LIMIT),
        name="outproj",
    )(h, yab, yc, p, wo, plew, pleg, fng)


def _block_diag(w):
    g, a, b = w.shape
    out = jnp.zeros((g * a, g * b), w.dtype)
    for i in range(g):
        out = out.at[i * a:(i + 1) * a, i * b:(i + 1) * b].set(w[i])
    return out


def kernel(x, p, norm_g, w_in, b_in, conv_dw_w, conv_dw_b, conv_ln_g, conv_ln_b, conv_pw_w,
           conv_pw_b, pool_w, pool_b, pool_scale, w_out, ple_w, ple_gate_w, final_norm_g):
    B, S, D = x.shape
    depth = w_in.shape[0]
    assert S % ROW_TILE == 0 and D == 1024 and w_in.shape[2] == N_MAIN + 324
    topk = min(TOPK_MAX, S // 4)

    half = HEAD_DIM // 2
    inv = ROPE_THETA ** (-jnp.arange(half, dtype=F32) / half)
    ang = jnp.arange(S).astype(F32)[:, None] * inv[None, :]
    cos = jnp.tile(jnp.cos(ang), (1, 4))
    sin = jnp.tile(jnp.concatenate([-jnp.sin(ang), jnp.sin(ang)], axis=1), (1, 2))

    row2 = lambda a: a.reshape(1, -1)
    h = x
    for i in range(depth):
        wm = w_in[i, :, :N_MAIN].astype(BF16)
        bm = row2(b_in[i, :N_MAIN])
        wx = jnp.pad(w_in[i, :, N_MAIN:], ((0, 0), (0, IDX_PAD - 324)))
        bx = row2(jnp.pad(b_in[i, N_MAIN:], (0, IDX_PAD - 324)))
        zc, zp, qt, k, vt, ag, qi, ws, wt = _inproj(
            h, row2(norm_g[i]), wm, bm, wx, bx, cos, sin)
        yab = _convpool(zc, zp, conv_dw_w[i], row2(conv_dw_b[i]), row2(conv_ln_g[i]),
                        row2(conv_ln_b[i]), conv_pw_w[i].astype(BF16), row2(conv_pw_b[i]),
                        _block_diag(pool_w[i]).astype(BF16), row2(pool_b[i]), row2(pool_scale[i]))
        yc = _attention(qt, qi, wt, ag, ws, k, vt, topk)
        h = _outproj(h, yab, yc, p[i], w_out[i].astype(BF16), ple_w[i].astype(BF16),
                     ple_gate_w[i].astype(BF16), row2(final_norm_g), final=(i == depth - 1))
    return h
```

```python
import functools

import jax
import jax.numpy as jnp
from jax import lax
from jax.experimental import pallas as pl
from jax.experimental.pallas import tpu as pltpu

F32 = jnp.float32
BF16 = jnp.bfloat16

CONV_W = 256
CONV_K = 31
POOL_W = 256
POOL_GW = 64
POOL_WINDOWS = (2, 4, 8, 16)
HEAD_DIM = 64
ATTN_W = 512
N_HEADS = 8
IDX_HEADS = 4
IDX_DIM = 64
TOPK_MAX = 256
ROPE_THETA = 10000.0
EPS = 1e-6

SUBLANES = 8
Q_BLOCK = 128
KEY_CHUNK = 256
SCAN_ROWS = 2 * KEY_CHUNK
BIAS_UNROLL = 4
COUNT_ROWS = BIAS_UNROLL * KEY_CHUNK
PAIR_ROWS = 128 + 16
VT_ROWS = (N_HEADS // 2) * PAIR_ROWS
COARSE_PASSES = 9
BF16_STEP = 2.0 ** -6
TINY = 1e-30
FIRST_PASSES = 10
NEXT_PASSES = 2
INTERP_AFTER = 5
INTERP_CLIP = 0.25
LOG2E = 1.4426950408889634
HALO = 32
ROW_TILE = 512
IDX_PAD = 384
N_MAIN = 3 * CONV_W + 2 * POOL_W + 4 * ATTN_W
MASK_BIAS = -1e30
NEED_ALL = 1e9
VMEM_LIMIT = 60 * 1024 * 1024
HIGHEST = lax.Precision.HIGHEST


def _sigmoid(x):
    return 1.0 / (1.0 + jnp.exp(-x))


def _silu(x):
    return x * _sigmoid(x)


def _dot(a, b, precision=None):
    return lax.dot_general(a, b, (((1,), (0,)), ((), ())), precision=precision,
                           preferred_element_type=F32)


def _rope_slab(x, cos, sin):
    lane = lax.broadcasted_iota(jnp.int32, x.shape, 1)
    swapped = jnp.where((lane & 32) == 0, pltpu.roll(x, 96, 1), pltpu.roll(x, 32, 1))
    return x * cos + swapped * sin


def _rope(x, cos, sin):
    return jnp.concatenate(
        [_rope_slab(x[:, s:s + 128], cos, sin) for s in range(0, x.shape[1], 128)], axis=1)


def _inproj_kernel(h_ref, g_ref, wm_ref, bm_ref, wx_ref, bx_ref, cos_ref, sin_ref,
                   zc_ref, zp_ref, qt_ref, k_ref, vt_ref, ag_ref, qi_ref, ws_ref, wt_ref):
    x = h_ref[0]
    n = x * lax.rsqrt(jnp.mean(x * x, axis=-1, keepdims=True) + EPS) * g_ref[...]
    nb = n.astype(BF16)
    cos = cos_ref[...]
    sin = sin_ref[...]

    def proj(a, b):
        return _dot(nb, wm_ref[:, a:b]) + bm_ref[:, a:b]

    zc_ref[0] = proj(0, 768)
    zp_ref[0] = proj(768, 1280)
    qt = (_rope(proj(1280, 1792), cos, sin) * (HEAD_DIM ** -0.5 * LOG2E)).T.astype(BF16)
    for jb in range(ROW_TILE // Q_BLOCK):
        qt_ref[0, jb] = qt[:, jb * Q_BLOCK:(jb + 1) * Q_BLOCK]
    k_ref[0] = _rope(proj(1792, 2304), cos, sin).astype(BF16)
    vt = proj(2304, 2816).T.astype(BF16)
    ones = jnp.ones((PAIR_ROWS - 128, KEY_CHUNK), BF16)
    for c in range(ROW_TILE // KEY_CHUNK):
        vt_ref[0, c] = jnp.concatenate(
            [piece for pr in range(N_HEADS // 2)
             for piece in (vt[pr * 128:(pr + 1) * 128, c * KEY_CHUNK:(c + 1) * KEY_CHUNK], ones)], axis=0)
    ag_ref[0] = proj(2816, 3328)

    zx = _dot(n, wx_ref[...], precision=HIGHEST) + bx_ref[...]
    qit = (_rope(zx[:, 0:256], cos, sin) * (IDX_DIM ** -0.5)).T.astype(BF16)
    pad = jnp.zeros((128 - IDX_DIM, Q_BLOCK), BF16)
    for jb in range(ROW_TILE // Q_BLOCK):
        qi_ref[0, jb] = jnp.concatenate(
            [jnp.concatenate([qit[h * IDX_DIM:(h + 1) * IDX_DIM, jb * Q_BLOCK:(jb + 1) * Q_BLOCK],
                              pad], axis=0) for h in range(IDX_HEADS)], axis=1)
    tail = zx[:, 256:384]
    lane = lax.broadcasted_iota(jnp.int32, tail.shape, 1)
    tail = jnp.where(lane < IDX_DIM, _rope_slab(tail, cos, sin), tail)
    ws_ref[0] = tail.astype(BF16)
    wt = tail.T[IDX_DIM:IDX_DIM + SUBLANES, :]
    for jb in range(ROW_TILE // Q_BLOCK):
        wt_ref[0, jb] = wt[:, jb * Q_BLOCK:(jb + 1) * Q_BLOCK]


def _inproj(h, g, wm, bm, wx, bx, cos, sin):
    B, S, D = h.shape
    nt = S // ROW_TILE
    cpt = ROW_TILE // KEY_CHUNK
    qpt = ROW_TILE // Q_BLOCK
    nch = S // KEY_CHUNK
    nq = S // Q_BLOCK
    row = lambda w: pl.BlockSpec((1, ROW_TILE, w), lambda b, t: (b, t, 0))
    full = lambda a: pl.BlockSpec(a.shape, lambda b, t: (0,) * a.ndim)
    tab = pl.BlockSpec((ROW_TILE, 128), lambda b, t: (t, 0))
    qblk = lambda r, w: pl.BlockSpec((1, qpt, r, w), lambda b, t: (b, t, 0, 0))
    out_shape = (
        jax.ShapeDtypeStruct((B, S, 768), F32),
        jax.ShapeDtypeStruct((B, S, 512), F32),
        jax.ShapeDtypeStruct((B, nq, ATTN_W, Q_BLOCK), BF16),
        jax.ShapeDtypeStruct((B, S, ATTN_W), BF16),
        jax.ShapeDtypeStruct((B, nch, VT_ROWS, KEY_CHUNK), BF16),
        jax.ShapeDtypeStruct((B, S, ATTN_W), F32),
        jax.ShapeDtypeStruct((B, nq, 128, IDX_HEADS * Q_BLOCK), BF16),
        jax.ShapeDtypeStruct((B, S, 128), BF16),
        jax.ShapeDtypeStruct((B, nq, SUBLANES, Q_BLOCK), F32),
    )
    out_specs = (
        row(768), row(512), qblk(ATTN_W, Q_BLOCK), row(ATTN_W),
        pl.BlockSpec((1, cpt, VT_ROWS, KEY_CHUNK), lambda b, t: (b, t, 0, 0)),
        row(ATTN_W), qblk(128, IDX_HEADS * Q_BLOCK), row(128), qblk(SUBLANES, Q_BLOCK),
    )
    return pl.pallas_call(
        _inproj_kernel,
        grid=(B, nt),
        in_specs=[row(D), full(g), full(wm), full(bm), full(wx), full(bx), tab, tab],
        out_specs=out_specs,
        out_shape=out_shape,
        compiler_params=pltpu.CompilerParams(
            dimension_semantics=("arbitrary", "arbitrary"), vmem_limit_bytes=VMEM_LIMIT),
        name="inproj",
    )(h, g, wm, bm, wx, bx, cos, sin)


def _convpool_kernel(zc_ref, zch_ref, zp_ref, zph_ref, dw_ref, dwb_ref, lng_ref, lnb_ref,
                     pw_ref, pwb_ref, plw_ref, plb_ref, pls_ref, o_ref, ubuf, xbuf):
    t = pl.program_id(1)
    keep = (t > 0).astype(F32)

    zc = zc_ref[0]
    zch = zch_ref[0]
    ubuf[0:HALO] = zch[:, 0:256] * _sigmoid(zch[:, 256:512]) * keep
    ubuf[HALO:HALO + ROW_TILE] = zc[:, 0:256] * _sigmoid(zc[:, 256:512])
    acc = jnp.zeros((ROW_TILE, CONV_W), F32) + dwb_ref[...]
    for j in range(CONV_K):
        acc = acc + ubuf[pl.ds(HALO - (CONV_K - 1) + j, ROW_TILE), :] * dw_ref[j:j + 1, :]
    mu = jnp.mean(acc, axis=-1, keepdims=True)
    xc = acc - mu
    y = xc * lax.rsqrt(jnp.mean(xc * xc, axis=-1, keepdims=True) + EPS) * lng_ref[...] + lnb_ref[...]
    y = _silu(y)
    ya = (_dot(y.astype(BF16), pw_ref[...]) + pwb_ref[...]) * _silu(zc[:, 512:768])

    zp = zp_ref[0]
    x = zp[:, 0:256]
    xbuf[0:HALO] = zph_ref[0][:, 0:256] * keep
    xbuf[HALO:HALO + ROW_TILE] = x
    run = x
    sums = {}
    for d in range(1, POOL_WINDOWS[-1]):
        run = run + xbuf[pl.ds(HALO - d, ROW_TILE), :]
        if d + 1 in POOL_WINDOWS:
            sums[d + 1] = run
    lane = lax.broadcasted_iota(jnp.int32, (ROW_TILE, POOL_W), 1)
    grp = lane // POOL_GW
    wsum = jnp.where(grp == 0, sums[2], jnp.where(grp == 1, sums[4],
                                                  jnp.where(grp == 2, sums[8], sums[16])))
    win = jnp.where(grp == 0, 2, jnp.where(grp == 1, 4, jnp.where(grp == 2, 8, 16)))
    pos = t * ROW_TILE + lax.broadcasted_iota(jnp.int32, (ROW_TILE, POOL_W), 0)
    cnt = jnp.minimum(pos + 1, win).astype(F32)
    d = wsum / cnt - x
    yb = (_dot(d.astype(BF16), plw_ref[...]) + plb_ref[...]) * pls_ref[...] * _silu(zp[:, 256:512])

    o_ref[0] = jnp.concatenate([ya, yb], axis=1)


def _convpool(zc, zp, dw, dwb, lng, lnb, pw, pwb, plw, plb, pls):
    B, S, _ = zc.shape
    nt = S // ROW_TILE
    hpt = ROW_TILE // HALO
    row = lambda w: pl.BlockSpec((1, ROW_TILE, w), lambda b, t: (b, t, 0))
    halo = lambda w: pl.BlockSpec((1, HALO, w), lambda b, t: (b, jnp.maximum(t * hpt - 1, 0), 0))
    full = lambda a: pl.BlockSpec(a.shape, lambda b, t: (0,) * a.ndim)
    return pl.pallas_call(
        _convpool_kernel,
        grid=(B, nt),
        in_specs=[row(768), halo(768), row(512), halo(512), full(dw), full(dwb), full(lng),
                  full(lnb), full(pw), full(pwb), full(plw), full(plb), full(pls)],
        out_specs=row(512),
        out_shape=jax.ShapeDtypeStruct((B, S, 512), F32),
        scratch_shapes=[pltpu.VMEM((HALO + ROW_TILE, CONV_W), F32),
                        pltpu.VMEM((HALO + ROW_TILE, POOL_W), F32)],
        compiler_params=pltpu.CompilerParams(
            dimension_semantics=("arbitrary", "arbitrary"), vmem_limit_bytes=VMEM_LIMIT),
        name="convpool",
    )(zc, zc, zp, zp, dw, dwb, lng, lnb, pw, pwb, plw, plb, pls)


def _fold_rows(x, op, fn=lambda p: p):
    return _combine([fn(x[r:r + SUBLANES, :]) for r in range(0, x.shape[0], SUBLANES)], op)


def _combine(parts, op):
    while len(parts) > 1:
        parts = [op(parts[i], parts[i + 1]) for i in range(0, len(parts) - 1, 2)] + (
            [parts[-1]] if len(parts) % 2 else [])
    return parts[0]


def _attn_kernel(qt_ref, qi_ref, wt_ref, ag_ref, ws_ref, k_ref, vt_ref, o_ref,
                 sc_ref, y_ref, qp_ref, acc_ref, lga_ref, lgb_ref, p_ref, *, topk):
    j = pl.program_id(1)
    nfull = j // 2
    nch = nfull + 1
    Q = Q_BLOCK
    KC = KEY_CHUNK
    qpos = j * Q + lax.broadcasted_iota(jnp.int32, (1, Q), 1)

    qi = qi_ref[0, 0]
    wt = wt_ref[0, 0]
    w_h = [wt[h:h + 1, :] for h in range(IDX_HEADS)]

    def unit_scores(u):
        r0 = pl.multiple_of(u * SCAN_ROWS, SCAN_ROWS)
        s = _dot(ws_ref[0, pl.ds(r0, SCAN_ROWS), :], qi)
        tot = jnp.zeros((SCAN_ROWS, Q), F32)
        for h in range(IDX_HEADS):
            tot = tot + w_h[h] * jnp.maximum(s[:, h * Q:(h + 1) * Q], 0.0)
        return tot

    def score_body(i, carry):
        mx, mn = carry
        for u in (2 * i, 2 * i + 1):
            s = unit_scores(u)
            sc_ref[pl.ds(pl.multiple_of(u * SCAN_ROWS, SCAN_ROWS), SCAN_ROWS), :] = s
            y_ref[pl.ds(pl.multiple_of(u * SCAN_ROWS, SCAN_ROWS), SCAN_ROWS), :] = s.astype(BF16)
            mx = jnp.maximum(mx, _fold_rows(s, jnp.maximum))
            mn = jnp.minimum(mn, _fold_rows(s, jnp.minimum))
        return mx, mn

    nscan = (nch + 1) // 2
    nbias = (nch + BIAS_UNROLL - 1) // BIAS_UNROLL
    init = (jnp.full((SUBLANES, Q), -jnp.inf, F32), jnp.full((SUBLANES, Q), jnp.inf, F32))
    mx, mn = lax.fori_loop(0, nbias, score_body, init)
    rowmax = jnp.max(mx, axis=0, keepdims=True)
    rowmin = jnp.min(mn, axis=0, keepdims=True)

    d0 = pl.multiple_of(nfull * KC, KC)
    kpos = nfull * KC + lax.broadcasted_iota(jnp.int32, (KC, Q), 0)
    sc_ref[pl.ds(d0, KC), :] = jnp.where(kpos <= qpos, sc_ref[pl.ds(d0, KC), :], -jnp.inf)
    y_ref[pl.ds(d0, KC), :] = sc_ref[pl.ds(d0, KC), :].astype(BF16)

    def pad_body(c, _):
        sc_ref[pl.ds(pl.multiple_of(c * KC, KC), KC), :] = jnp.full((KC, Q), -jnp.inf, F32)
        y_ref[pl.ds(pl.multiple_of(c * KC, KC), KC), :] = jnp.full((KC, Q), -jnp.inf, BF16)
        return 0

    lax.fori_loop(nch, nbias * BIAS_UNROLL, pad_body, 0)

    ncausal = (qpos + 1).astype(F32)
    kk = jnp.minimum(ncausal, float(topk))

    def count_ge(t):
        tb = jnp.broadcast_to(t, (SUBLANES, Q))

        def body(c, acc):
            sv = sc_ref[pl.ds(pl.multiple_of(c * SCAN_ROWS, SCAN_ROWS), SCAN_ROWS), :]
            return acc + _fold_rows(sv, jnp.add, lambda p: jnp.where(p >= tb, 1.0, 0.0))

        acc = lax.fori_loop(0, nscan, body, jnp.zeros((SUBLANES, Q), F32))
        return jnp.sum(acc, axis=0, keepdims=True)

    def bracket_ends(lo, hi):
        lob = jnp.broadcast_to(lo, (SUBLANES, Q))
        hib = jnp.broadcast_to(hi, (SUBLANES, Q))

        def body(c, carry):
            a, b = carry
            sv = sc_ref[pl.ds(pl.multiple_of(c * SCAN_ROWS, SCAN_ROWS), SCAN_ROWS), :]
            a = jnp.minimum(a, _fold_rows(sv, jnp.minimum, lambda p: jnp.where(p >= lob, p, jnp.inf)))
            b = jnp.maximum(b, _fold_rows(sv, jnp.maximum, lambda p: jnp.where(p < hib, p, -jnp.inf)))
            return a, b

        a, b = lax.fori_loop(0, nscan, body, (jnp.full((SUBLANES, Q), jnp.inf, F32),
                                            jnp.full((SUBLANES, Q), -jnp.inf, F32)))
        return jnp.min(a, axis=0, keepdims=True), jnp.max(b, axis=0, keepdims=True)

    res0 = (ncausal <= kk).astype(F32)

    def count_ge_coarse(gf):
        gb = jnp.broadcast_to(gf, (2 * SUBLANES, Q)).astype(BF16)
        one = jnp.ones((), BF16)
        zero = jnp.zeros((), BF16)

        def body(c, acc):
            yv = y_ref[pl.ds(pl.multiple_of(c * COUNT_ROWS, COUNT_ROWS), COUNT_ROWS), :]
            parts = [jnp.where(yv[r:r + 2 * SUBLANES, :] >= gb, one, zero)
                     for r in range(0, COUNT_ROWS, 2 * SUBLANES)]
            return acc + _combine(parts, jnp.add).astype(F32)

        acc = lax.fori_loop(0, nbias, body, jnp.zeros((2 * SUBLANES, Q), F32))
        return jnp.sum(acc, axis=0, keepdims=True)

    def coarse_pass(_, carry):
        lo, hi, first = carry
        mid = jnp.where(first > 0, rowmax, jnp.where(hi < jnp.inf, 0.5 * (lo + hi), lo))
        gf = mid.astype(BF16).astype(F32)
        below = count_ge_coarse(gf) < kk
        step_below = gf - jnp.abs(gf) * BF16_STEP - TINY
        return (jnp.where(below, lo, jnp.maximum(lo, step_below)), jnp.where(below, gf, hi),
                jnp.zeros((1, Q), F32))

    lo0, hi0, _ = lax.fori_loop(0, COARSE_PASSES, coarse_pass,
                                (rowmin, jnp.full((1, Q), jnp.inf, F32), jnp.ones((1, Q), F32)))
    state0 = dict(lo=lo0, hi=hi0, c_lo=ncausal,
                  c_hi=jnp.zeros((1, Q), F32), res=res0, thr=rowmin,
                  need=jnp.full((1, Q), NEED_ALL, F32), pivot=hi0,
                  use_pivot=jnp.ones((1, Q), F32), npass=jnp.zeros((1, Q), F32))

    def bisect_pass(_, st):
        lo, hi = st["lo"], st["hi"]
        bounded = hi < jnp.inf
        frac = jnp.clip((st["c_lo"] - kk + 0.5) / jnp.maximum(st["c_lo"] - st["c_hi"], 1.0),
                        INTERP_CLIP, 1.0 - INTERP_CLIP)
        frac = jnp.where(st["npass"] >= INTERP_AFTER, frac, 0.5)
        mid = jnp.where(st["use_pivot"] > 0, st["pivot"],
                        jnp.where(bounded, lo + (hi - lo) * frac, lo))
        cnt = count_ge(mid)
        live = st["res"] == 0
        ge = cnt >= kk
        hit = live & (cnt == kk)
        up = live & ge
        dn = live & jnp.logical_not(ge)
        return dict(
            lo=jnp.where(up, mid, lo), c_lo=jnp.where(up, cnt, st["c_lo"]),
            hi=jnp.where(dn, mid, hi), c_hi=jnp.where(dn, cnt, st["c_hi"]),
            res=jnp.where(hit, 1.0, st["res"]), thr=jnp.where(hit, mid, st["thr"]),
            need=st["need"], pivot=st["pivot"], use_pivot=jnp.zeros((1, Q), F32),
            npass=st["npass"] + 1.0)

    def round_body(carry):
        it, st = carry
        st = lax.fori_loop(0, jnp.where(it == 0, FIRST_PASSES, NEXT_PASSES), bisect_pass, st)
        a, b = bracket_ends(st["lo"], st["hi"])
        live = st["res"] == 0
        tie = live & (a == b)
        st = dict(st)
        last = live & (kk - st["c_hi"] == 1.0)
        st["thr"] = jnp.where(tie, a, jnp.where(last, b, st["thr"]))
        st["need"] = jnp.where(tie | last, kk - st["c_hi"], st["need"])
        st["res"] = jnp.where(tie | last, 1.0, st["res"])
        st["lo"] = jnp.where(live, a, st["lo"])
        st["pivot"] = b
        st["use_pivot"] = jnp.ones((1, Q), F32)
        return it + 1, st

    def round_cond(carry):
        _, st = carry
        return jnp.min(st["res"]) == 0

    _, st = lax.while_loop(round_cond, round_body, (jnp.int32(0), state0))
    thr = st["thr"]
    need0 = st["need"]

    qt = qt_ref[0, 0]
    rowhead = lax.broadcasted_iota(jnp.int32, (256, Q), 0) // HEAD_DIM
    zero = jnp.zeros((), BF16)
    for pr in range(N_HEADS // 2):
        qg = qt[(pr // 2) * 256:(pr // 2 + 1) * 256, :]
        qp_ref[pr] = jnp.concatenate([jnp.where(rowhead == (2 * pr) % 4, qg, zero),
                                      jnp.where(rowhead == (2 * pr + 1) % 4, qg, zero)], axis=1)
    acc_ref[...] = jnp.zeros(acc_ref.shape, F32)
    ri = lax.broadcasted_iota(jnp.int32, (KC, KC), 0)
    ci = lax.broadcasted_iota(jnp.int32, (KC, KC), 1)
    tril = jnp.where(ci <= ri, 1.0, 0.0).astype(BF16)

    def bias_body(u, need):
        for i in range(BIAS_UNROLL):
            r0 = pl.multiple_of((u * BIAS_UNROLL + i) * KC, KC)
            sv = sc_ref[pl.ds(r0, KC), :]
            eq = sv == thr
            prefix = _dot(tril, jnp.where(eq, 1.0, 0.0).astype(BF16))
            take_tie = jnp.where(prefix <= need, 0.0, MASK_BIAS)
            sc_ref[pl.ds(r0, KC), :] = jnp.where(sv > thr, 0.0, jnp.where(eq, take_tie, MASK_BIAS))
            ties = _fold_rows(sv, jnp.add, lambda p: jnp.where(p == thr, 1.0, 0.0))
            need = need - jnp.sum(ties, axis=0, keepdims=True)
        return need

    lax.fori_loop(0, nbias, bias_body, need0)

    def logits_into(c, lg_ref):
        r0 = pl.multiple_of(c * KC, KC)
        bias = sc_ref[pl.ds(r0, KC), :]
        bias2 = jnp.concatenate([bias, bias], axis=1)
        for pr in range(N_HEADS // 2):
            g0 = (pr // 2) * 256
            lg_ref[pr] = _dot(k_ref[0, pl.ds(r0, KC), g0:g0 + 256], qp_ref[pr]) + bias2

    def chunk_max(lg_ref):
        return tuple(jnp.max(_fold_rows(lg_ref[pr], jnp.maximum), axis=0, keepdims=True)
                     for pr in range(N_HEADS // 2))

    def softmax_step(c, lg_ref, ms, ls, cmax):
        new_ms, new_ls = [], []
        for pr in range(N_HEADS // 2):
            m_new = jnp.maximum(ms[pr], cmax[pr])
            alpha = jnp.exp2(ms[pr] - m_new)
            for r in range(0, KC, 2 * SUBLANES):
                pe = jnp.exp2(lg_ref[pr, r:r + 2 * SUBLANES, :] - m_new)
                p_ref[pr, r:r + 2 * SUBLANES, :] = pe.astype(BF16)
            o = _dot(vt_ref[0, c, pr * PAIR_ROWS:(pr + 1) * PAIR_ROWS, :], p_ref[pr])
            new_ls.append(alpha * ls[pr] + o[128:129, :])
            new_ms.append(m_new)
            a0 = pr * 128
            acc_ref[a0:a0 + 64, :] = acc_ref[a0:a0 + 64, :] * alpha[:, 0:Q] + o[0:64, 0:Q]
            acc_ref[a0 + 64:a0 + 128, :] = (acc_ref[a0 + 64:a0 + 128, :] * alpha[:, Q:2 * Q]
                                            + o[64:128, Q:2 * Q])
        return tuple(new_ms), tuple(new_ls)

    logits_into(0, lga_ref)

    def attn_body(i, carry):
        ms, ls, cmax = carry
        c = 2 * i
        logits_into(c + 1, lgb_ref)
        ms, ls = softmax_step(c, lga_ref, ms, ls, cmax)
        logits_into(jnp.minimum(c + 2, 2 * nscan - 1), lga_ref)
        ms, ls = softmax_step(c + 1, lgb_ref, ms, ls, chunk_max(lgb_ref))
        return ms, ls, chunk_max(lga_ref)

    ms0 = tuple(jnp.full((1, 2 * Q), MASK_BIAS, F32) for _ in range(N_HEADS // 2))
    ls0 = tuple(jnp.zeros((1, 2 * Q), F32) for _ in range(N_HEADS // 2))
    _, ls, _ = lax.fori_loop(0, nscan, attn_body, (ms0, ls0, chunk_max(lga_ref)))
    outs = []
    for pr in range(N_HEADS // 2):
        a0 = pr * 128
        outs.append(acc_ref[a0:a0 + 64, :] / ls[pr][:, 0:Q])
        outs.append(acc_ref[a0 + 64:a0 + 128, :] / ls[pr][:, Q:2 * Q])
    o_ref[0] = jnp.concatenate(outs, axis=0).T * _silu(ag_ref[0])


def _attention(qt, qi, wt, ag, ws, k, vt, topk):
    B, S, _ = k.shape
    nq = S // Q_BLOCK
    nch = S // KEY_CHUNK
    qblk = lambda r, w: pl.BlockSpec((1, 1, r, w), lambda b, j: (b, j, 0, 0))
    once = pl.Buffered(1)
    seq = lambda w: pl.BlockSpec((1, S, w), lambda b, j: (b, 0, 0), pipeline_mode=once)
    return pl.pallas_call(
        functools.partial(_attn_kernel, topk=topk),
        grid=(B, nq),
        in_specs=[
            qblk(ATTN_W, Q_BLOCK), qblk(128, IDX_HEADS * Q_BLOCK), qblk(SUBLANES, Q_BLOCK),
            pl.BlockSpec((1, Q_BLOCK, ATTN_W), lambda b, j: (b, j, 0)),
            seq(128), seq(ATTN_W),
            pl.BlockSpec((1, nch, VT_ROWS, KEY_CHUNK), lambda b, j: (b, 0, 0, 0), pipeline_mode=once),
        ],
        out_specs=pl.BlockSpec((1, Q_BLOCK, ATTN_W), lambda b, j: (b, j, 0)),
        out_shape=jax.ShapeDtypeStruct((B, S, ATTN_W), F32),
        scratch_shapes=[
            pltpu.VMEM((S, Q_BLOCK), F32),
            pltpu.VMEM((S, Q_BLOCK), BF16),
            pltpu.VMEM((N_HEADS // 2, 256, 2 * Q_BLOCK), BF16),
            pltpu.VMEM((ATTN_W, Q_BLOCK), F32),
            pltpu.VMEM((N_HEADS // 2, KEY_CHUNK, 2 * Q_BLOCK), F32),
            pltpu.VMEM((N_HEADS // 2, KEY_CHUNK, 2 * Q_BLOCK), F32),
            pltpu.VMEM((N_HEADS // 2, KEY_CHUNK, 2 * Q_BLOCK), BF16),
        ],
        compiler_params=pltpu.CompilerParams(
            dimension_semantics=("arbitrary", "arbitrary"), vmem_limit_bytes=VMEM_LIMIT),
        name="attention",
    )(qt, qi, wt, ag, ws, k, vt)


def _outproj_kernel(h_ref, yab_ref, yc_ref, p_ref, wo_ref, plew_ref, pleg_ref, fng_ref, o_ref,
                    *, final):
    h = (h_ref[0] + _dot(yab_ref[0].astype(BF16), wo_ref[0:512, :])
         + _dot(yc_ref[0].astype(BF16), wo_ref[512:1024, :]))
    ple = _dot(p_ref[0].astype(BF16), plew_ref[...])
    h = h + ple * _sigmoid(_dot(h.astype(BF16), pleg_ref[...]))
    if final:
        h = h * lax.rsqrt(jnp.mean(h * h, axis=-1, keepdims=True) + EPS) * fng_ref[...]
    o_ref[0] = h


def _outproj(h, yab, yc, p, wo, plew, pleg, fng, final):
    B, S, D = h.shape
    nt = S // ROW_TILE
    row = lambda w: pl.BlockSpec((1, ROW_TILE, w), lambda b, t: (b, t, 0))
    full = lambda a: pl.BlockSpec(a.shape, lambda b, t: (0,) * a.ndim)
    return pl.pallas_call(
        functools.partial(_outproj_kernel, final=final),
        grid=(B, nt),
        in_specs=[row(D), row(512), row(512), row(p.shape[-1]), full(wo), full(plew), full(pleg),
                  full(fng)],
        out_specs=row(D),
        out_shape=jax.ShapeDtypeStruct((B, S, D), F32),
        compiler_params=pltpu.CompilerParams(
            dimension_semantics=("arbitrary", "arbitrary"), vmem_limit_bytes=VMEM_LIMIT),
        name="outproj",
    )(h, yab, yc, p, wo, plew, pleg, fng)


def _block_diag(w):
    g, a, b = w.shape
    out = jnp.zeros((g * a, g * b), w.dtype)
    for i in range(g):
        out = out.at[i * a:(i + 1) * a, i * b:(i + 1) * b].set(w[i])
    return out


def kernel(x, p, norm_g, w_in, b_in, conv_dw_w, conv_dw_b, conv_ln_g, conv_ln_b, conv_pw_w,
           conv_pw_b, pool_w, pool_b, pool_scale, w_out, ple_w, ple_gate_w, final_norm_g):
    B, S, D = x.shape
    depth = w_in.shape[0]
    assert S % ROW_TILE == 0 and D == 1024 and w_in.shape[2] == N_MAIN + 324
    topk = min(TOPK_MAX, S // 4)

    half = HEAD_DIM // 2
    inv = ROPE_THETA ** (-jnp.arange(half, dtype=F32) / half)
    ang = jnp.arange(S).astype(F32)[:, None] * inv[None, :]
    cos = jnp.tile(jnp.cos(ang), (1, 4))
    sin = jnp.tile(jnp.concatenate([-jnp.sin(ang), jnp.sin(ang)], axis=1), (1, 2))

    row2 = lambda a: a.reshape(1, -1)
    h = x
    for i in range(depth):
        wm = w_in[i, :, :N_MAIN].astype(BF16)
        bm = row2(b_in[i, :N_MAIN])
        wx = jnp.pad(w_in[i, :, N_MAIN:], ((0, 0), (0, IDX_PAD - 324)))
        bx = row2(jnp.pad(b_in[i, N_MAIN:], (0, IDX_PAD - 324)))
        zc, zp, qt, k, vt, ag, qi, ws, wt = _inproj(
            h, row2(norm_g[i]), wm, bm, wx, bx, cos, sin)
        yab = _convpool(zc, zp, conv_dw_w[i], row2(conv_dw_b[i]), row2(conv_ln_g[i]),
                        row2(conv_ln_b[i]), conv_pw_w[i].astype(BF16), row2(conv_pw_b[i]),
                        _block_diag(pool_w[i]).astype(BF16), row2(pool_b[i]), row2(pool_scale[i]))
        yc = _attention(qt, qi, wt, ag, ws, k, vt, topk)
        h = _outproj(h, yab, yc, p[i], w_out[i].astype(BF16), ple_w[i].astype(BF16),
                     ple_gate_w[i].astype(BF16), row2(final_norm_g), final=(i == depth - 1))
    return h
```

```python
import functools

import jax
import jax.numpy as jnp
from jax import lax
from jax.experimental import pallas as pl
from jax.experimental.pallas import tpu as pltpu

F32 = jnp.float32
BF16 = jnp.bfloat16

CONV_W = 256
CONV_K = 31
POOL_W = 256
POOL_GW = 64
POOL_WINDOWS = (2, 4, 8, 16)
HEAD_DIM = 64
ATTN_W = 512
N_HEADS = 8
IDX_HEADS = 4
IDX_DIM = 64
TOPK_MAX = 256
ROPE_THETA = 10000.0
EPS = 1e-6

SUBLANES = 8
Q_BLOCK = 128
KEY_CHUNK = 256
SCAN_ROWS = 2 * KEY_CHUNK
BIAS_UNROLL = 4
PAIR_ROWS = 128 + 16
VT_ROWS = (N_HEADS // 2) * PAIR_ROWS
FIRST_PASSES = 16
NEXT_PASSES = 2
INTERP_AFTER = 5
INTERP_CLIP = 0.25
LOG2E = 1.4426950408889634
HALO = 32
ROW_TILE = 512
IDX_PAD = 384
N_MAIN = 3 * CONV_W + 2 * POOL_W + 4 * ATTN_W
MASK_BIAS = -1e30
NEED_ALL = 1e9
VMEM_LIMIT = 60 * 1024 * 1024


def _sigmoid(x):
    return 1.0 / (1.0 + jnp.exp(-x))


def _silu(x):
    return x * _sigmoid(x)


def _dot(a, b):
    return lax.dot_general(a, b, (((1,), (0,)), ((), ())), preferred_element_type=F32)


def _rope_slab(x, cos, sin):
    lane = lax.broadcasted_iota(jnp.int32, x.shape, 1)
    swapped = jnp.where((lane & 32) == 0, pltpu.roll(x, 96, 1), pltpu.roll(x, 32, 1))
    return x * cos + swapped * sin


def _rope(x, cos, sin):
    return jnp.concatenate(
        [_rope_slab(x[:, s:s + 128], cos, sin) for s in range(0, x.shape[1], 128)], axis=1)


def _inproj_kernel(h_ref, g_ref, wm_ref, bm_ref, wxh_ref, wxl_ref, bx_ref, cos_ref, sin_ref,
                   zc_ref, zp_ref, qt_ref, k_ref, vt_ref, ag_ref, qi_ref, ws_ref, wt_ref):
    x = h_ref[0]
    n = x * lax.rsqrt(jnp.mean(x * x, axis=-1, keepdims=True) + EPS) * g_ref[...]
    nb = n.astype(BF16)
    cos = cos_ref[...]
    sin = sin_ref[...]

    def proj(a, b):
        return _dot(nb, wm_ref[:, a:b]) + bm_ref[:, a:b]

    zc_ref[0] = proj(0, 768)
    zp_ref[0] = proj(768, 1280)
    qt = (_rope(proj(1280, 1792), cos, sin) * (HEAD_DIM ** -0.5 * LOG2E)).T.astype(BF16)
    for jb in range(ROW_TILE // Q_BLOCK):
        qt_ref[0, jb] = qt[:, jb * Q_BLOCK:(jb + 1) * Q_BLOCK]
    k_ref[0] = _rope(proj(1792, 2304), cos, sin).astype(BF16)
    vt = proj(2304, 2816).T.astype(BF16)
    ones = jnp.ones((PAIR_ROWS - 128, KEY_CHUNK), BF16)
    for c in range(ROW_TILE // KEY_CHUNK):
        vt_ref[0, c] = jnp.concatenate(
            [piece for pr in range(N_HEADS // 2)
             for piece in (vt[pr * 128:(pr + 1) * 128, c * KEY_CHUNK:(c + 1) * KEY_CHUNK], ones)], axis=0)
    ag_ref[0] = proj(2816, 3328)

    nl = (n - nb.astype(F32)).astype(BF16)
    zx = (_dot(nb, wxh_ref[...]) + _dot(nb, wxl_ref[...]) + _dot(nl, wxh_ref[...])) + bx_ref[...]
    qit = (_rope(zx[:, 0:256], cos, sin) * (IDX_DIM ** -0.5)).T.astype(BF16)
    pad = jnp.zeros((128 - IDX_DIM, Q_BLOCK), BF16)
    for jb in range(ROW_TILE // Q_BLOCK):
        qi_ref[0, jb] = jnp.concatenate(
            [jnp.concatenate([qit[h * IDX_DIM:(h + 1) * IDX_DIM, jb * Q_BLOCK:(jb + 1) * Q_BLOCK],
                              pad], axis=0) for h in range(IDX_HEADS)], axis=1)
    tail = zx[:, 256:384]
    lane = lax.broadcasted_iota(jnp.int32, tail.shape, 1)
    tail = jnp.where(lane < IDX_DIM, _rope_slab(tail, cos, sin), tail)
    ws_ref[0] = tail.astype(BF16)
    wt = tail.T[IDX_DIM:IDX_DIM + SUBLANES, :]
    for jb in range(ROW_TILE // Q_BLOCK):
        wt_ref[0, jb] = wt[:, jb * Q_BLOCK:(jb + 1) * Q_BLOCK]


def _inproj(h, g, wm, bm, wxh, wxl, bx, cos, sin):
    B, S, D = h.shape
    nt = S // ROW_TILE
    cpt = ROW_TILE // KEY_CHUNK
    qpt = ROW_TILE // Q_BLOCK
    nch = S // KEY_CHUNK
    nq = S // Q_BLOCK
    row = lambda w: pl.BlockSpec((1, ROW_TILE, w), lambda b, t: (b, t, 0))
    full = lambda a: pl.BlockSpec(a.shape, lambda b, t: (0,) * a.ndim)
    tab = pl.BlockSpec((ROW_TILE, 128), lambda b, t: (t, 0))
    qblk = lambda r, w: pl.BlockSpec((1, qpt, r, w), lambda b, t: (b, t, 0, 0))
    out_shape = (
        jax.ShapeDtypeStruct((B, S, 768), F32),
        jax.ShapeDtypeStruct((B, S, 512), F32),
        jax.ShapeDtypeStruct((B, nq, ATTN_W, Q_BLOCK), BF16),
        jax.ShapeDtypeStruct((B, S, ATTN_W), BF16),
        jax.ShapeDtypeStruct((B, nch, VT_ROWS, KEY_CHUNK), BF16),
        jax.ShapeDtypeStruct((B, S, ATTN_W), F32),
        jax.ShapeDtypeStruct((B, nq, 128, IDX_HEADS * Q_BLOCK), BF16),
        jax.ShapeDtypeStruct((B, S, 128), BF16),
        jax.ShapeDtypeStruct((B, nq, SUBLANES, Q_BLOCK), F32),
    )
    out_specs = (
        row(768), row(512), qblk(ATTN_W, Q_BLOCK), row(ATTN_W),
        pl.BlockSpec((1, cpt, VT_ROWS, KEY_CHUNK), lambda b, t: (b, t, 0, 0)),
        row(ATTN_W), qblk(128, IDX_HEADS * Q_BLOCK), row(128), qblk(SUBLANES, Q_BLOCK),
    )
    return pl.pallas_call(
        _inproj_kernel,
        grid=(B, nt),
        in_specs=[row(D), full(g), full(wm), full(bm), full(wxh), full(wxl), full(bx), tab, tab],
        out_specs=out_specs,
        out_shape=out_shape,
        compiler_params=pltpu.CompilerParams(
            dimension_semantics=("arbitrary", "arbitrary"), vmem_limit_bytes=VMEM_LIMIT),
        name="inproj",
    )(h, g, wm, bm, wxh, wxl, bx, cos, sin)


def _convpool_kernel(zc_ref, zch_ref, zp_ref, zph_ref, dw_ref, dwb_ref, lng_ref, lnb_ref,
                     pw_ref, pwb_ref, plw_ref, plb_ref, pls_ref, o_ref, ubuf, xbuf):
    t = pl.program_id(1)
    keep = (t > 0).astype(F32)

    zc = zc_ref[0]
    zch = zch_ref[0]
    ubuf[0:HALO] = zch[:, 0:256] * _sigmoid(zch[:, 256:512]) * keep
    ubuf[HALO:HALO + ROW_TILE] = zc[:, 0:256] * _sigmoid(zc[:, 256:512])
    acc = jnp.zeros((ROW_TILE, CONV_W), F32) + dwb_ref[...]
    for j in range(CONV_K):
        acc = acc + ubuf[pl.ds(HALO - (CONV_K - 1) + j, ROW_TILE), :] * dw_ref[j:j + 1, :]
    mu = jnp.mean(acc, axis=-1, keepdims=True)
    xc = acc - mu
    y = xc * lax.rsqrt(jnp.mean(xc * xc, axis=-1, keepdims=True) + EPS) * lng_ref[...] + lnb_ref[...]
    y = _silu(y)
    ya = (_dot(y.astype(BF16), pw_ref[...]) + pwb_ref[...]) * _silu(zc[:, 512:768])

    zp = zp_ref[0]
    x = zp[:, 0:256]
    xbuf[0:HALO] = zph_ref[0][:, 0:256] * keep
    xbuf[HALO:HALO + ROW_TILE] = x
    run = x
    sums = {}
    for d in range(1, POOL_WINDOWS[-1]):
        run = run + xbuf[pl.ds(HALO - d, ROW_TILE), :]
        if d + 1 in POOL_WINDOWS:
            sums[d + 1] = run
    lane = lax.broadcasted_iota(jnp.int32, (ROW_TILE, POOL_W), 1)
    grp = lane // POOL_GW
    wsum = jnp.where(grp == 0, sums[2], jnp.where(grp == 1, sums[4],
                                                  jnp.where(grp == 2, sums[8], sums[16])))
    win = jnp.where(grp == 0, 2, jnp.where(grp == 1, 4, jnp.where(grp == 2, 8, 16)))
    pos = t * ROW_TILE + lax.broadcasted_iota(jnp.int32, (ROW_TILE, POOL_W), 0)
    cnt = jnp.minimum(pos + 1, win).astype(F32)
    d = wsum / cnt - x
    yb = (_dot(d.astype(BF16), plw_ref[...]) + plb_ref[...]) * pls_ref[...] * _silu(zp[:, 256:512])

    o_ref[0] = jnp.concatenate([ya, yb], axis=1)


def _convpool(zc, zp, dw, dwb, lng, lnb, pw, pwb, plw, plb, pls):
    B, S, _ = zc.shape
    nt = S // ROW_TILE
    hpt = ROW_TILE // HALO
    row = lambda w: pl.BlockSpec((1, ROW_TILE, w), lambda b, t: (b, t, 0))
    halo = lambda w: pl.BlockSpec((1, HALO, w), lambda b, t: (b, jnp.maximum(t * hpt - 1, 0), 0))
    full = lambda a: pl.BlockSpec(a.shape, lambda b, t: (0,) * a.ndim)
    return pl.pallas_call(
        _convpool_kernel,
        grid=(B, nt),
        in_specs=[row(768), halo(768), row(512), halo(512), full(dw), full(dwb), full(lng),
                  full(lnb), full(pw), full(pwb), full(plw), full(plb), full(pls)],
        out_specs=row(512),
        out_shape=jax.ShapeDtypeStruct((B, S, 512), F32),
        scratch_shapes=[pltpu.VMEM((HALO + ROW_TILE, CONV_W), F32),
                        pltpu.VMEM((HALO + ROW_TILE, POOL_W), F32)],
        compiler_params=pltpu.CompilerParams(
            dimension_semantics=("arbitrary", "arbitrary"), vmem_limit_bytes=VMEM_LIMIT),
        name="convpool",
    )(zc, zc, zp, zp, dw, dwb, lng, lnb, pw, pwb, plw, plb, pls)


def _fold_rows(x, op, fn=lambda p: p):
    return _combine([fn(x[r:r + SUBLANES, :]) for r in range(0, x.shape[0], SUBLANES)], op)


def _combine(parts, op):
    while len(parts) > 1:
        parts = [op(parts[i], parts[i + 1]) for i in range(0, len(parts) - 1, 2)] + (
            [parts[-1]] if len(parts) % 2 else [])
    return parts[0]


def _attn_kernel(qt_ref, qi_ref, wt_ref, ag_ref, ws_ref, k_ref, vt_ref, o_ref,
                 sc_ref, qp_ref, acc_ref, lga_ref, lgb_ref, p_ref, *, topk):
    j = pl.program_id(1)
    nfull = j // 2
    nch = nfull + 1
    Q = Q_BLOCK
    KC = KEY_CHUNK
    qpos = j * Q + lax.broadcasted_iota(jnp.int32, (1, Q), 1)

    qi = qi_ref[0, 0]
    wt = wt_ref[0, 0]
    w_h = [wt[h:h + 1, :] for h in range(IDX_HEADS)]

    def unit_scores(u):
        r0 = pl.multiple_of(u * SCAN_ROWS, SCAN_ROWS)
        s = _dot(ws_ref[0, pl.ds(r0, SCAN_ROWS), :], qi)
        tot = jnp.zeros((SCAN_ROWS, Q), F32)
        for h in range(IDX_HEADS):
            tot = tot + w_h[h] * jnp.maximum(s[:, h * Q:(h + 1) * Q], 0.0)
        return tot

    def score_body(i, carry):
        mx, mn = carry
        for u in (2 * i, 2 * i + 1):
            s = unit_scores(u)
            sc_ref[pl.ds(pl.multiple_of(u * SCAN_ROWS, SCAN_ROWS), SCAN_ROWS), :] = s
            mx = jnp.maximum(mx, _fold_rows(s, jnp.maximum))
            mn = jnp.minimum(mn, _fold_rows(s, jnp.minimum))
        return mx, mn

    nscan = (nch + 1) // 2
    nbias = (nch + BIAS_UNROLL - 1) // BIAS_UNROLL
    init = (jnp.full((SUBLANES, Q), -jnp.inf, F32), jnp.full((SUBLANES, Q), jnp.inf, F32))
    mx, mn = lax.fori_loop(0, nbias, score_body, init)
    rowmax = jnp.max(mx, axis=0, keepdims=True)
    rowmin = jnp.min(mn, axis=0, keepdims=True)

    d0 = pl.multiple_of(nfull * KC, KC)
    kpos = nfull * KC + lax.broadcasted_iota(jnp.int32, (KC, Q), 0)
    sc_ref[pl.ds(d0, KC), :] = jnp.where(kpos <= qpos, sc_ref[pl.ds(d0, KC), :], -jnp.inf)

    def pad_body(c, _):
        sc_ref[pl.ds(pl.multiple_of(c * KC, KC), KC), :] = jnp.full((KC, Q), -jnp.inf, F32)
        return 0

    lax.fori_loop(nch, nbias * BIAS_UNROLL, pad_body, 0)

    ncausal = (qpos + 1).astype(F32)
    kk = jnp.minimum(ncausal, float(topk))

    def count_ge(t):
        tb = jnp.broadcast_to(t, (SUBLANES, Q))

        def body(c, acc):
            sv = sc_ref[pl.ds(pl.multiple_of(c * SCAN_ROWS, SCAN_ROWS), SCAN_ROWS), :]
            return acc + _fold_rows(sv, jnp.add, lambda p: jnp.where(p >= tb, 1.0, 0.0))

        acc = lax.fori_loop(0, nscan, body, jnp.zeros((SUBLANES, Q), F32))
        return jnp.sum(acc, axis=0, keepdims=True)

    def bracket_ends(lo, hi):
        lob = jnp.broadcast_to(lo, (SUBLANES, Q))
        hib = jnp.broadcast_to(hi, (SUBLANES, Q))

        def body(c, carry):
            a, b = carry
            sv = sc_ref[pl.ds(pl.multiple_of(c * SCAN_ROWS, SCAN_ROWS), SCAN_ROWS), :]
            a = jnp.minimum(a, _fold_rows(sv, jnp.minimum, lambda p: jnp.where(p >= lob, p, jnp.inf)))
            b = jnp.maximum(b, _fold_rows(sv, jnp.maximum, lambda p: jnp.where(p < hib, p, -jnp.inf)))
            return a, b

        a, b = lax.fori_loop(0, nscan, body, (jnp.full((SUBLANES, Q), jnp.inf, F32),
                                            jnp.full((SUBLANES, Q), -jnp.inf, F32)))
        return jnp.min(a, axis=0, keepdims=True), jnp.max(b, axis=0, keepdims=True)

    res0 = (ncausal <= kk).astype(F32)
    state0 = dict(lo=rowmin, hi=jnp.full((1, Q), jnp.inf, F32), c_lo=ncausal,
                  c_hi=jnp.zeros((1, Q), F32), res=res0, thr=rowmin,
                  need=jnp.full((1, Q), NEED_ALL, F32), pivot=rowmax,
                  use_pivot=jnp.ones((1, Q), F32), npass=jnp.zeros((1, Q), F32))

    def bisect_pass(_, st):
        lo, hi = st["lo"], st["hi"]
        bounded = hi < jnp.inf
        frac = jnp.clip((st["c_lo"] - kk + 0.5) / jnp.maximum(st["c_lo"] - st["c_hi"], 1.0),
                        INTERP_CLIP, 1.0 - INTERP_CLIP)
        frac = jnp.where(st["npass"] >= INTERP_AFTER, frac, 0.5)
        mid = jnp.where(st["use_pivot"] > 0, st["pivot"],
                        jnp.where(bounded, lo + (hi - lo) * frac, lo))
        cnt = count_ge(mid)
        live = st["res"] == 0
        ge = cnt >= kk
        hit = live & (cnt == kk)
        up = live & ge
        dn = live & jnp.logical_not(ge)
        return dict(
            lo=jnp.where(up, mid, lo), c_lo=jnp.where(up, cnt, st["c_lo"]),
            hi=jnp.where(dn, mid, hi), c_hi=jnp.where(dn, cnt, st["c_hi"]),
            res=jnp.where(hit, 1.0, st["res"]), thr=jnp.where(hit, mid, st["thr"]),
            need=st["need"], pivot=st["pivot"], use_pivot=jnp.zeros((1, Q), F32),
            npass=st["npass"] + 1.0)

    def round_body(carry):
        it, st = carry
        st = lax.fori_loop(0, jnp.where(it == 0, FIRST_PASSES, NEXT_PASSES), bisect_pass, st)
        a, b = bracket_ends(st["lo"], st["hi"])
        live = st["res"] == 0
        tie = live & (a == b)
        st = dict(st)
        last = live & (kk - st["c_hi"] == 1.0)
        st["thr"] = jnp.where(tie, a, jnp.where(last, b, st["thr"]))
        st["need"] = jnp.where(tie | last, kk - st["c_hi"], st["need"])
        st["res"] = jnp.where(tie | last, 1.0, st["res"])
        st["lo"] = jnp.where(live, a, st["lo"])
        st["pivot"] = b
        st["use_pivot"] = jnp.ones((1, Q), F32)
        return it + 1, st

    def round_cond(carry):
        _, st = carry
        return jnp.min(st["res"]) == 0

    _, st = lax.while_loop(round_cond, round_body, (jnp.int32(0), state0))
    thr = st["thr"]
    need0 = st["need"]

    qt = qt_ref[0, 0]
    rowhead = lax.broadcasted_iota(jnp.int32, (256, Q), 0) // HEAD_DIM
    zero = jnp.zeros((), BF16)
    for pr in range(N_HEADS // 2):
        qg = qt[(pr // 2) * 256:(pr // 2 + 1) * 256, :]
        qp_ref[pr] = jnp.concatenate([jnp.where(rowhead == (2 * pr) % 4, qg, zero),
                                      jnp.where(rowhead == (2 * pr + 1) % 4, qg, zero)], axis=1)
    acc_ref[...] = jnp.zeros(acc_ref.shape, F32)
    ri = lax.broadcasted_iota(jnp.int32, (KC, KC), 0)
    ci = lax.broadcasted_iota(jnp.int32, (KC, KC), 1)
    tril = jnp.where(ci <= ri, 1.0, 0.0).astype(BF16)

    def bias_body(u, need):
        for i in range(BIAS_UNROLL):
            r0 = pl.multiple_of((u * BIAS_UNROLL + i) * KC, KC)
            sv = sc_ref[pl.ds(r0, KC), :]
            eq = sv == thr
            prefix = _dot(tril, jnp.where(eq, 1.0, 0.0).astype(BF16))
            take_tie = jnp.where(prefix <= need, 0.0, MASK_BIAS)
            sc_ref[pl.ds(r0, KC), :] = jnp.where(sv > thr, 0.0, jnp.where(eq, take_tie, MASK_BIAS))
            ties = _fold_rows(sv, jnp.add, lambda p: jnp.where(p == thr, 1.0, 0.0))
            need = need - jnp.sum(ties, axis=0, keepdims=True)
        return need

    lax.fori_loop(0, nbias, bias_body, need0)

    def logits_into(c, lg_ref):
        r0 = pl.multiple_of(c * KC, KC)
        bias = sc_ref[pl.ds(r0, KC), :]
        bias2 = jnp.concatenate([bias, bias], axis=1)
        for pr in range(N_HEADS // 2):
            g0 = (pr // 2) * 256
            lg_ref[pr] = _dot(k_ref[0, pl.ds(r0, KC), g0:g0 + 256], qp_ref[pr]) + bias2

    def chunk_max(lg_ref):
        return tuple(jnp.max(_fold_rows(lg_ref[pr], jnp.maximum), axis=0, keepdims=True)
                     for pr in range(N_HEADS // 2))

    def softmax_step(c, lg_ref, ms, ls, cmax):
        new_ms, new_ls = [], []
        for pr in range(N_HEADS // 2):
            m_new = jnp.maximum(ms[pr], cmax[pr])
            alpha = jnp.exp2(ms[pr] - m_new)
            for r in range(0, KC, 2 * SUBLANES):
                pe = jnp.exp2(lg_ref[pr, r:r + 2 * SUBLANES, :] - m_new)
                p_ref[pr, r:r + 2 * SUBLANES, :] = pe.astype(BF16)
            o = _dot(vt_ref[0, c, pr * PAIR_ROWS:(pr + 1) * PAIR_ROWS, :], p_ref[pr])
            new_ls.append(alpha * ls[pr] + o[128:129, :])
            new_ms.append(m_new)
            a0 = pr * 128
            acc_ref[a0:a0 + 64, :] = acc_ref[a0:a0 + 64, :] * alpha[:, 0:Q] + o[0:64, 0:Q]
            acc_ref[a0 + 64:a0 + 128, :] = (acc_ref[a0 + 64:a0 + 128, :] * alpha[:, Q:2 * Q]
                                            + o[64:128, Q:2 * Q])
        return tuple(new_ms), tuple(new_ls)

    logits_into(0, lga_ref)

    def attn_body(i, carry):
        ms, ls, cmax = carry
        c = 2 * i
        logits_into(c + 1, lgb_ref)
        ms, ls = softmax_step(c, lga_ref, ms, ls, cmax)
        logits_into(jnp.minimum(c + 2, 2 * nscan - 1), lga_ref)
        ms, ls = softmax_step(c + 1, lgb_ref, ms, ls, chunk_max(lgb_ref))
        return ms, ls, chunk_max(lga_ref)

    ms0 = tuple(jnp.full((1, 2 * Q), MASK_BIAS, F32) for _ in range(N_HEADS // 2))
    ls0 = tuple(jnp.zeros((1, 2 * Q), F32) for _ in range(N_HEADS // 2))
    _, ls, _ = lax.fori_loop(0, nscan, attn_body, (ms0, ls0, chunk_max(lga_ref)))
    outs = []
    for pr in range(N_HEADS // 2):
        a0 = pr * 128
        outs.append(acc_ref[a0:a0 + 64, :] / ls[pr][:, 0:Q])
        outs.append(acc_ref[a0 + 64:a0 + 128, :] / ls[pr][:, Q:2 * Q])
    o_ref[0] = jnp.concatenate(outs, axis=0).T * _silu(ag_ref[0])


def _attention(qt, qi, wt, ag, ws, k, vt, topk):
    B, S, _ = k.shape
    nq = S // Q_BLOCK
    nch = S // KEY_CHUNK
    qblk = lambda r, w: pl.BlockSpec((1, 1, r, w), lambda b, j: (b, j, 0, 0))
    once = pl.Buffered(1)
    seq = lambda w: pl.BlockSpec((1, S, w), lambda b, j: (b, 0, 0), pipeline_mode=once)
    return pl.pallas_call(
        functools.partial(_attn_kernel, topk=topk),
        grid=(B, nq),
        in_specs=[
            qblk(ATTN_W, Q_BLOCK), qblk(128, IDX_HEADS * Q_BLOCK), qblk(SUBLANES, Q_BLOCK),
            pl.BlockSpec((1, Q_BLOCK, ATTN_W), lambda b, j: (b, j, 0)),
            seq(128), seq(ATTN_W),
            pl.BlockSpec((1, nch, VT_ROWS, KEY_CHUNK), lambda b, j: (b, 0, 0, 0), pipeline_mode=once),
        ],
        out_specs=pl.BlockSpec((1, Q_BLOCK, ATTN_W), lambda b, j: (b, j, 0)),
        out_shape=jax.ShapeDtypeStruct((B, S, ATTN_W), F32),
        scratch_shapes=[
            pltpu.VMEM((S, Q_BLOCK), F32),
            pltpu.VMEM((N_HEADS // 2, 256, 2 * Q_BLOCK), BF16),
            pltpu.VMEM((ATTN_W, Q_BLOCK), F32),
            pltpu.VMEM((N_HEADS // 2, KEY_CHUNK, 2 * Q_BLOCK), F32),
            pltpu.VMEM((N_HEADS // 2, KEY_CHUNK, 2 * Q_BLOCK), F32),
            pltpu.VMEM((N_HEADS // 2, KEY_CHUNK, 2 * Q_BLOCK), BF16),
        ],
        compiler_params=pltpu.CompilerParams(
            dimension_semantics=("arbitrary", "arbitrary"), vmem_limit_bytes=VMEM_LIMIT),
        name="attention",
    )(qt, qi, wt, ag, ws, k, vt)


def _outproj_kernel(h_ref, yab_ref, yc_ref, p_ref, wo_ref, plew_ref, pleg_ref, fng_ref, o_ref,
                    *, final):
    h = (h_ref[0] + _dot(yab_ref[0].astype(BF16), wo_ref[0:512, :])
         + _dot(yc_ref[0].astype(BF16), wo_ref[512:1024, :]))
    ple = _dot(p_ref[0].astype(BF16), plew_ref[...])
    h = h + ple * _sigmoid(_dot(h.astype(BF16), pleg_ref[...]))
    if final:
        h = h * lax.rsqrt(jnp.mean(h * h, axis=-1, keepdims=True) + EPS) * fng_ref[...]
    o_ref[0] = h


def _outproj(h, yab, yc, p, wo, plew, pleg, fng, final):
    B, S, D = h.shape
    nt = S // ROW_TILE
    row = lambda w: pl.BlockSpec((1, ROW_TILE, w), lambda b, t: (b, t, 0))
    full = lambda a: pl.BlockSpec(a.shape, lambda b, t: (0,) * a.ndim)
    return pl.pallas_call(
        functools.partial(_outproj_kernel, final=final),
        grid=(B, nt),
        in_specs=[row(D), row(512), row(512), row(p.shape[-1]), full(wo), full(plew), full(pleg),
                  full(fng)],
        out_specs=row(D),
        out_shape=jax.ShapeDtypeStruct((B, S, D), F32),
        compiler_params=pltpu.CompilerParams(
            dimension_semantics=("arbitrary", "arbitrary"), vmem_limit_bytes=VMEM_LIMIT),
        name="outproj",
    )(h, yab, yc, p, wo, plew, pleg, fng)


def _block_diag(w):
    g, a, b = w.shape
    out = jnp.zeros((g * a, g * b), w.dtype)
    for i in range(g):
        out = out.at[i * a:(i + 1) * a, i * b:(i + 1) * b].set(w[i])
    return out


def kernel(x, p, norm_g, w_in, b_in, conv_dw_w, conv_dw_b, conv_ln_g, conv_ln_b, conv_pw_w,
           conv_pw_b, pool_w, pool_b, pool_scale, w_out, ple_w, ple_gate_w, final_norm_g):
    B, S, D = x.shape
    depth = w_in.shape[0]
    assert S % ROW_TILE == 0 and D == 1024 and w_in.shape[2] == N_MAIN + 324
    topk = min(TOPK_MAX, S // 4)

    half = HEAD_DIM // 2
    inv = ROPE_THETA ** (-jnp.arange(half, dtype=F32) / half)
    ang = jnp.arange(S).astype(F32)[:, None] * inv[None, :]
    cos = jnp.tile(jnp.cos(ang), (1, 4))
    sin = jnp.tile(jnp.concatenate([-jnp.sin(ang), jnp.sin(ang)], axis=1), (1, 2))

    row2 = lambda a: a.reshape(1, -1)
    h = x
    for i in range(depth):
        wm = w_in[i, :, :N_MAIN].astype(BF16)
        bm = row2(b_in[i, :N_MAIN])
        wx = jnp.pad(w_in[i, :, N_MAIN:], ((0, 0), (0, IDX_PAD - 324)))
        wxh = wx.astype(BF16)
        bx = row2(jnp.pad(b_in[i, N_MAIN:], (0, IDX_PAD - 324)))
        zc, zp, qt, k, vt, ag, qi, ws, wt = _inproj(
            h, row2(norm_g[i]), wm, bm, wxh, (wx - wxh.astype(F32)).astype(BF16), bx, cos, sin)
        yab = _convpool(zc, zp, conv_dw_w[i], row2(conv_dw_b[i]), row2(conv_ln_g[i]),
                        row2(conv_ln_b[i]), conv_pw_w[i].astype(BF16), row2(conv_pw_b[i]),
                        _block_diag(pool_w[i]).astype(BF16), row2(pool_b[i]), row2(pool_scale[i]))
        yc = _attention(qt, qi, wt, ag, ws, k, vt, topk)
        h = _outproj(h, yab, yc, p[i], w_out[i].astype(BF16), ple_w[i].astype(BF16),
                     ple_gate_w[i].astype(BF16), row2(final_norm_g), final=(i == depth - 1))
    return h
```

```python
import functools

import jax
import jax.numpy as jnp
from jax import lax
from jax.experimental import pallas as pl
from jax.experimental.pallas import tpu as pltpu

F32 = jnp.float32
BF16 = jnp.bfloat16

CONV_W = 256
CONV_K = 31
POOL_W = 256
POOL_GW = 64
POOL_WINDOWS = (2, 4, 8, 16)
HEAD_DIM = 64
ATTN_W = 512
N_HEADS = 8
IDX_HEADS = 4
IDX_DIM = 64
TOPK_MAX = 256
ROPE_THETA = 10000.0
EPS = 1e-6

SUBLANES = 8
Q_BLOCK = 128
KEY_CHUNK = 256
SCAN_ROWS = 2 * KEY_CHUNK
BIAS_UNROLL = 4
PAIR_ROWS = 128 + 16
VT_ROWS = (N_HEADS // 2) * PAIR_ROWS
FIRST_PASSES = 16
NEXT_PASSES = 2
INTERP_AFTER = 5
INTERP_CLIP = 0.25
LOG2E = 1.4426950408889634
HALO = 32
ROW_TILE = 512
IDX_PAD = 384
N_MAIN = 3 * CONV_W + 2 * POOL_W + 4 * ATTN_W
MASK_BIAS = -1e30
NEED_ALL = 1e9
VMEM_LIMIT = 60 * 1024 * 1024


def _sigmoid(x):
    return 1.0 / (1.0 + jnp.exp(-x))


def _silu(x):
    return x * _sigmoid(x)


def _dot(a, b):
    return lax.dot_general(a, b, (((1,), (0,)), ((), ())), preferred_element_type=F32)


def _rope_slab(x, cos, sin):
    lane = lax.broadcasted_iota(jnp.int32, x.shape, 1)
    swapped = jnp.where((lane & 32) == 0, pltpu.roll(x, 96, 1), pltpu.roll(x, 32, 1))
    return x * cos + swapped * sin


def _rope(x, cos, sin):
    return jnp.concatenate(
        [_rope_slab(x[:, s:s + 128], cos, sin) for s in range(0, x.shape[1], 128)], axis=1)


def _inproj_kernel(h_ref, g_ref, wm_ref, bm_ref, wxh_ref, wxl_ref, bx_ref, cos_ref, sin_ref,
                   zc_ref, zp_ref, qt_ref, k_ref, vt_ref, ag_ref, qi_ref, ws_ref, wt_ref):
    x = h_ref[0]
    n = x * lax.rsqrt(jnp.mean(x * x, axis=-1, keepdims=True) + EPS) * g_ref[...]
    nb = n.astype(BF16)
    cos = cos_ref[...]
    sin = sin_ref[...]

    def proj(a, b):
        return _dot(nb, wm_ref[:, a:b]) + bm_ref[:, a:b]

    zc_ref[0] = proj(0, 768)
    zp_ref[0] = proj(768, 1280)
    qt = (_rope(proj(1280, 1792), cos, sin) * (HEAD_DIM ** -0.5 * LOG2E)).T.astype(BF16)
    for jb in range(ROW_TILE // Q_BLOCK):
        qt_ref[0, jb] = qt[:, jb * Q_BLOCK:(jb + 1) * Q_BLOCK]
    k_ref[0] = _rope(proj(1792, 2304), cos, sin).astype(BF16)
    vt = proj(2304, 2816).T.astype(BF16)
    ones = jnp.ones((PAIR_ROWS - 128, KEY_CHUNK), BF16)
    for c in range(ROW_TILE // KEY_CHUNK):
        vt_ref[0, c] = jnp.concatenate(
            [piece for pr in range(N_HEADS // 2)
             for piece in (vt[pr * 128:(pr + 1) * 128, c * KEY_CHUNK:(c + 1) * KEY_CHUNK], ones)], axis=0)
    ag_ref[0] = proj(2816, 3328)

    nl = (n - nb.astype(F32)).astype(BF16)
    zx = (_dot(nb, wxh_ref[...]) + _dot(nb, wxl_ref[...]) + _dot(nl, wxh_ref[...])) + bx_ref[...]
    qit = (_rope(zx[:, 0:256], cos, sin) * (IDX_DIM ** -0.5)).T.astype(BF16)
    pad = jnp.zeros((128 - IDX_DIM, Q_BLOCK), BF16)
    for jb in range(ROW_TILE // Q_BLOCK):
        qi_ref[0, jb] = jnp.concatenate(
            [jnp.concatenate([qit[h * IDX_DIM:(h + 1) * IDX_DIM, jb * Q_BLOCK:(jb + 1) * Q_BLOCK],
                              pad], axis=0) for h in range(IDX_HEADS)], axis=1)
    tail = zx[:, 256:384]
    lane = lax.broadcasted_iota(jnp.int32, tail.shape, 1)
    tail = jnp.where(lane < IDX_DIM, _rope_slab(tail, cos, sin), tail)
    ws_ref[0] = tail.astype(BF16)
    wt = tail.T[IDX_DIM:IDX_DIM + SUBLANES, :]
    for jb in range(ROW_TILE // Q_BLOCK):
        wt_ref[0, jb] = wt[:, jb * Q_BLOCK:(jb + 1) * Q_BLOCK]


def _inproj(h, g, wm, bm, wxh, wxl, bx, cos, sin):
    B, S, D = h.shape
    nt = S // ROW_TILE
    cpt = ROW_TILE // KEY_CHUNK
    qpt = ROW_TILE // Q_BLOCK
    nch = S // KEY_CHUNK
    nq = S // Q_BLOCK
    row = lambda w: pl.BlockSpec((1, ROW_TILE, w), lambda b, t: (b, t, 0))
    full = lambda a: pl.BlockSpec(a.shape, lambda b, t: (0,) * a.ndim)
    tab = pl.BlockSpec((ROW_TILE, 128), lambda b, t: (t, 0))
    qblk = lambda r, w: pl.BlockSpec((1, qpt, r, w), lambda b, t: (b, t, 0, 0))
    out_shape = (
        jax.ShapeDtypeStruct((B, S, 768), F32),
        jax.ShapeDtypeStruct((B, S, 512), F32),
        jax.ShapeDtypeStruct((B, nq, ATTN_W, Q_BLOCK), BF16),
        jax.ShapeDtypeStruct((B, S, ATTN_W), BF16),
        jax.ShapeDtypeStruct((B, nch, VT_ROWS, KEY_CHUNK), BF16),
        jax.ShapeDtypeStruct((B, S, ATTN_W), F32),
        jax.ShapeDtypeStruct((B, nq, 128, IDX_HEADS * Q_BLOCK), BF16),
        jax.ShapeDtypeStruct((B, S, 128), BF16),
        jax.ShapeDtypeStruct((B, nq, SUBLANES, Q_BLOCK), F32),
    )
    out_specs = (
        row(768), row(512), qblk(ATTN_W, Q_BLOCK), row(ATTN_W),
        pl.BlockSpec((1, cpt, VT_ROWS, KEY_CHUNK), lambda b, t: (b, t, 0, 0)),
        row(ATTN_W), qblk(128, IDX_HEADS * Q_BLOCK), row(128), qblk(SUBLANES, Q_BLOCK),
    )
    return pl.pallas_call(
        _inproj_kernel,
        grid=(B, nt),
        in_specs=[row(D), full(g), full(wm), full(bm), full(wxh), full(wxl), full(bx), tab, tab],
        out_specs=out_specs,
        out_shape=out_shape,
        compiler_params=pltpu.CompilerParams(
            dimension_semantics=("arbitrary", "arbitrary"), vmem_limit_bytes=VMEM_LIMIT),
        name="inproj",
    )(h, g, wm, bm, wxh, wxl, bx, cos, sin)


def _convpool_kernel(zc_ref, zch_ref, zp_ref, zph_ref, dw_ref, dwb_ref, lng_ref, lnb_ref,
                     pw_ref, pwb_ref, plw_ref, plb_ref, pls_ref, o_ref, ubuf, xbuf, shift):
    t = pl.program_id(1)
    keep = (t > 0).astype(F32)

    zc = zc_ref[0]
    zch = zch_ref[0]
    ubuf[0:HALO] = zch[:, 0:256] * _sigmoid(zch[:, 256:512]) * keep
    ubuf[HALO:HALO + ROW_TILE] = zc[:, 0:256] * _sigmoid(zc[:, 256:512])
    def window(buf, off):
        r = off % SUBLANES
        return (buf[off:off + ROW_TILE, :] if r == 0
                else shift[r - 1, off - r:off - r + ROW_TILE, :])

    def make_shifts(buf):
        for r in range(1, SUBLANES):
            shift[r - 1] = buf[pl.ds(r, HALO + ROW_TILE - SUBLANES), :]

    make_shifts(ubuf)
    acc = jnp.zeros((ROW_TILE, CONV_W), F32) + dwb_ref[...]
    for j in range(CONV_K):
        acc = acc + window(ubuf, HALO - (CONV_K - 1) + j) * dw_ref[j:j + 1, :]
    mu = jnp.mean(acc, axis=-1, keepdims=True)
    xc = acc - mu
    y = xc * lax.rsqrt(jnp.mean(xc * xc, axis=-1, keepdims=True) + EPS) * lng_ref[...] + lnb_ref[...]
    y = _silu(y)
    ya = (_dot(y.astype(BF16), pw_ref[...]) + pwb_ref[...]) * _silu(zc[:, 512:768])

    zp = zp_ref[0]
    x = zp[:, 0:256]
    xbuf[0:HALO] = zph_ref[0][:, 0:256] * keep
    xbuf[HALO:HALO + ROW_TILE] = x
    make_shifts(xbuf)
    run = x
    sums = {}
    for d in range(1, POOL_WINDOWS[-1]):
        run = run + window(xbuf, HALO - d)
        if d + 1 in POOL_WINDOWS:
            sums[d + 1] = run
    lane = lax.broadcasted_iota(jnp.int32, (ROW_TILE, POOL_W), 1)
    grp = lane // POOL_GW
    wsum = jnp.where(grp == 0, sums[2], jnp.where(grp == 1, sums[4],
                                                  jnp.where(grp == 2, sums[8], sums[16])))
    win = jnp.where(grp == 0, 2, jnp.where(grp == 1, 4, jnp.where(grp == 2, 8, 16)))
    pos = t * ROW_TILE + lax.broadcasted_iota(jnp.int32, (ROW_TILE, POOL_W), 0)
    cnt = jnp.minimum(pos + 1, win).astype(F32)
    d = wsum / cnt - x
    yb = (_dot(d.astype(BF16), plw_ref[...]) + plb_ref[...]) * pls_ref[...] * _silu(zp[:, 256:512])

    o_ref[0] = jnp.concatenate([ya, yb], axis=1)


def _convpool(zc, zp, dw, dwb, lng, lnb, pw, pwb, plw, plb, pls):
    B, S, _ = zc.shape
    nt = S // ROW_TILE
    hpt = ROW_TILE // HALO
    row = lambda w: pl.BlockSpec((1, ROW_TILE, w), lambda b, t: (b, t, 0))
    halo = lambda w: pl.BlockSpec((1, HALO, w), lambda b, t: (b, jnp.maximum(t * hpt - 1, 0), 0))
    full = lambda a: pl.BlockSpec(a.shape, lambda b, t: (0,) * a.ndim)
    return pl.pallas_call(
        _convpool_kernel,
        grid=(B, nt),
        in_specs=[row(768), halo(768), row(512), halo(512), full(dw), full(dwb), full(lng),
                  full(lnb), full(pw), full(pwb), full(plw), full(plb), full(pls)],
        out_specs=row(512),
        out_shape=jax.ShapeDtypeStruct((B, S, 512), F32),
        scratch_shapes=[pltpu.VMEM((HALO + ROW_TILE, CONV_W), F32),
                        pltpu.VMEM((HALO + ROW_TILE, POOL_W), F32),
                        pltpu.VMEM((SUBLANES - 1, HALO + ROW_TILE - SUBLANES, CONV_W), F32)],
        compiler_params=pltpu.CompilerParams(
            dimension_semantics=("arbitrary", "arbitrary"), vmem_limit_bytes=VMEM_LIMIT),
        name="convpool",
    )(zc, zc, zp, zp, dw, dwb, lng, lnb, pw, pwb, plw, plb, pls)


def _fold_rows(x, op, fn=lambda p: p):
    return _combine([fn(x[r:r + SUBLANES, :]) for r in range(0, x.shape[0], SUBLANES)], op)


def _combine(parts, op):
    while len(parts) > 1:
        parts = [op(parts[i], parts[i + 1]) for i in range(0, len(parts) - 1, 2)] + (
            [parts[-1]] if len(parts) % 2 else [])
    return parts[0]


def _attn_kernel(qt_ref, qi_ref, wt_ref, ag_ref, ws_ref, k_ref, vt_ref, o_ref,
                 sc_ref, qp_ref, acc_ref, lga_ref, lgb_ref, p_ref, *, topk):
    j = pl.program_id(1)
    nfull = j // 2
    nch = nfull + 1
    Q = Q_BLOCK
    KC = KEY_CHUNK
    qpos = j * Q + lax.broadcasted_iota(jnp.int32, (1, Q), 1)

    qi = qi_ref[0, 0]
    wt = wt_ref[0, 0]
    w_h = [wt[h:h + 1, :] for h in range(IDX_HEADS)]

    def unit_scores(u):
        r0 = pl.multiple_of(u * SCAN_ROWS, SCAN_ROWS)
        s = _dot(ws_ref[0, pl.ds(r0, SCAN_ROWS), :], qi)
        tot = jnp.zeros((SCAN_ROWS, Q), F32)
        for h in range(IDX_HEADS):
            tot = tot + w_h[h] * jnp.maximum(s[:, h * Q:(h + 1) * Q], 0.0)
        return tot

    def score_body(i, carry):
        mx, mn = carry
        for u in (2 * i, 2 * i + 1):
            s = unit_scores(u)
            sc_ref[pl.ds(pl.multiple_of(u * SCAN_ROWS, SCAN_ROWS), SCAN_ROWS), :] = s
            mx = jnp.maximum(mx, _fold_rows(s, jnp.maximum))
            mn = jnp.minimum(mn, _fold_rows(s, jnp.minimum))
        return mx, mn

    nscan = (nch + 1) // 2
    nbias = (nch + BIAS_UNROLL - 1) // BIAS_UNROLL
    init = (jnp.full((SUBLANES, Q), -jnp.inf, F32), jnp.full((SUBLANES, Q), jnp.inf, F32))
    mx, mn = lax.fori_loop(0, nbias, score_body, init)
    rowmax = jnp.max(mx, axis=0, keepdims=True)
    rowmin = jnp.min(mn, axis=0, keepdims=True)

    d0 = pl.multiple_of(nfull * KC, KC)
    kpos = nfull * KC + lax.broadcasted_iota(jnp.int32, (KC, Q), 0)
    sc_ref[pl.ds(d0, KC), :] = jnp.where(kpos <= qpos, sc_ref[pl.ds(d0, KC), :], -jnp.inf)

    def pad_body(c, _):
        sc_ref[pl.ds(pl.multiple_of(c * KC, KC), KC), :] = jnp.full((KC, Q), -jnp.inf, F32)
        return 0

    lax.fori_loop(nch, nbias * BIAS_UNROLL, pad_body, 0)

    ncausal = (qpos + 1).astype(F32)
    kk = jnp.minimum(ncausal, float(topk))

    def count_ge(t):
        tb = jnp.broadcast_to(t, (SUBLANES, Q))

        def body(c, acc):
            sv = sc_ref[pl.ds(pl.multiple_of(c * SCAN_ROWS, SCAN_ROWS), SCAN_ROWS), :]
            return acc + _fold_rows(sv, jnp.add, lambda p: jnp.where(p >= tb, 1.0, 0.0))

        acc = lax.fori_loop(0, nscan, body, jnp.zeros((SUBLANES, Q), F32))
        return jnp.sum(acc, axis=0, keepdims=True)

    def bracket_ends(lo, hi):
        lob = jnp.broadcast_to(lo, (SUBLANES, Q))
        hib = jnp.broadcast_to(hi, (SUBLANES, Q))

        def body(c, carry):
            a, b = carry
            sv = sc_ref[pl.ds(pl.multiple_of(c * SCAN_ROWS, SCAN_ROWS), SCAN_ROWS), :]
            a = jnp.minimum(a, _fold_rows(sv, jnp.minimum, lambda p: jnp.where(p >= lob, p, jnp.inf)))
            b = jnp.maximum(b, _fold_rows(sv, jnp.maximum, lambda p: jnp.where(p < hib, p, -jnp.inf)))
            return a, b

        a, b = lax.fori_loop(0, nscan, body, (jnp.full((SUBLANES, Q), jnp.inf, F32),
                                            jnp.full((SUBLANES, Q), -jnp.inf, F32)))
        return jnp.min(a, axis=0, keepdims=True), jnp.max(b, axis=0, keepdims=True)

    res0 = (ncausal <= kk).astype(F32)
    state0 = dict(lo=rowmin, hi=jnp.full((1, Q), jnp.inf, F32), c_lo=ncausal,
                  c_hi=jnp.zeros((1, Q), F32), res=res0, thr=rowmin,
                  need=jnp.full((1, Q), NEED_ALL, F32), pivot=rowmax,
                  use_pivot=jnp.ones((1, Q), F32), npass=jnp.zeros((1, Q), F32))

    def bisect_pass(_, st):
        lo, hi = st["lo"], st["hi"]
        bounded = hi < jnp.inf
        frac = jnp.clip((st["c_lo"] - kk + 0.5) / jnp.maximum(st["c_lo"] - st["c_hi"], 1.0),
                        INTERP_CLIP, 1.0 - INTERP_CLIP)
        frac = jnp.where(st["npass"] >= INTERP_AFTER, frac, 0.5)
        mid = jnp.where(st["use_pivot"] > 0, st["pivot"],
                        jnp.where(bounded, lo + (hi - lo) * frac, lo))
        cnt = count_ge(mid)
        live = st["res"] == 0
        ge = cnt >= kk
        hit = live & (cnt == kk)
        up = live & ge
        dn = live & jnp.logical_not(ge)
        return dict(
            lo=jnp.where(up, mid, lo), c_lo=jnp.where(up, cnt, st["c_lo"]),
            hi=jnp.where(dn, mid, hi), c_hi=jnp.where(dn, cnt, st["c_hi"]),
            res=jnp.where(hit, 1.0, st["res"]), thr=jnp.where(hit, mid, st["thr"]),
            need=st["need"], pivot=st["pivot"], use_pivot=jnp.zeros((1, Q), F32),
            npass=st["npass"] + 1.0)

    def round_body(carry):
        it, st = carry
        st = lax.fori_loop(0, jnp.where(it == 0, FIRST_PASSES, NEXT_PASSES), bisect_pass, st)
        a, b = bracket_ends(st["lo"], st["hi"])
        live = st["res"] == 0
        tie = live & (a == b)
        st = dict(st)
        last = live & (kk - st["c_hi"] == 1.0)
        st["thr"] = jnp.where(tie, a, jnp.where(last, b, st["thr"]))
        st["need"] = jnp.where(tie | last, kk - st["c_hi"], st["need"])
        st["res"] = jnp.where(tie | last, 1.0, st["res"])
        st["lo"] = jnp.where(live, a, st["lo"])
        st["pivot"] = b
        st["use_pivot"] = jnp.ones((1, Q), F32)
        return it + 1, st

    def round_cond(carry):
        _, st = carry
        return jnp.min(st["res"]) == 0

    _, st = lax.while_loop(round_cond, round_body, (jnp.int32(0), state0))
    thr = st["thr"]
    need0 = st["need"]

    qt = qt_ref[0, 0]
    rowhead = lax.broadcasted_iota(jnp.int32, (256, Q), 0) // HEAD_DIM
    zero = jnp.zeros((), BF16)
    for pr in range(N_HEADS // 2):
        qg = qt[(pr // 2) * 256:(pr // 2 + 1) * 256, :]
        qp_ref[pr] = jnp.concatenate([jnp.where(rowhead == (2 * pr) % 4, qg, zero),
                                      jnp.where(rowhead == (2 * pr + 1) % 4, qg, zero)], axis=1)
    acc_ref[...] = jnp.zeros(acc_ref.shape, F32)
    ri = lax.broadcasted_iota(jnp.int32, (KC, KC), 0)
    ci = lax.broadcasted_iota(jnp.int32, (KC, KC), 1)
    tril = jnp.where(ci <= ri, 1.0, 0.0).astype(BF16)

    def bias_body(u, need):
        for i in range(BIAS_UNROLL):
            r0 = pl.multiple_of((u * BIAS_UNROLL + i) * KC, KC)
            sv = sc_ref[pl.ds(r0, KC), :]
            eq = sv == thr
            prefix = _dot(tril, jnp.where(eq, 1.0, 0.0).astype(BF16))
            take_tie = jnp.where(prefix <= need, 0.0, MASK_BIAS)
            sc_ref[pl.ds(r0, KC), :] = jnp.where(sv > thr, 0.0, jnp.where(eq, take_tie, MASK_BIAS))
            ties = _fold_rows(sv, jnp.add, lambda p: jnp.where(p == thr, 1.0, 0.0))
            need = need - jnp.sum(ties, axis=0, keepdims=True)
        return need

    lax.fori_loop(0, nbias, bias_body, need0)

    def logits_into(c, lg_ref):
        r0 = pl.multiple_of(c * KC, KC)
        bias = sc_ref[pl.ds(r0, KC), :]
        bias2 = jnp.concatenate([bias, bias], axis=1)
        for pr in range(N_HEADS // 2):
            g0 = (pr // 2) * 256
            lg_ref[pr] = _dot(k_ref[0, pl.ds(r0, KC), g0:g0 + 256], qp_ref[pr]) + bias2

    def chunk_max(lg_ref):
        return tuple(jnp.max(_fold_rows(lg_ref[pr], jnp.maximum), axis=0, keepdims=True)
                     for pr in range(N_HEADS // 2))

    def softmax_step(c, lg_ref, ms, ls, cmax):
        new_ms, new_ls = [], []
        for pr in range(N_HEADS // 2):
            m_new = jnp.maximum(ms[pr], cmax[pr])
            alpha = jnp.exp2(ms[pr] - m_new)
            for r in range(0, KC, 2 * SUBLANES):
                pe = jnp.exp2(lg_ref[pr, r:r + 2 * SUBLANES, :] - m_new)
                p_ref[pr, r:r + 2 * SUBLANES, :] = pe.astype(BF16)
            o = _dot(vt_ref[0, c, pr * PAIR_ROWS:(pr + 1) * PAIR_ROWS, :], p_ref[pr])
            new_ls.append(alpha * ls[pr] + o[128:129, :])
            new_ms.append(m_new)
            a0 = pr * 128
            acc_ref[a0:a0 + 64, :] = acc_ref[a0:a0 + 64, :] * alpha[:, 0:Q] + o[0:64, 0:Q]
            acc_ref[a0 + 64:a0 + 128, :] = (acc_ref[a0 + 64:a0 + 128, :] * alpha[:, Q:2 * Q]
                                            + o[64:128, Q:2 * Q])
        return tuple(new_ms), tuple(new_ls)

    logits_into(0, lga_ref)

    def attn_body(i, carry):
        ms, ls, cmax = carry
        c = 2 * i
        logits_into(c + 1, lgb_ref)
        ms, ls = softmax_step(c, lga_ref, ms, ls, cmax)
        logits_into(jnp.minimum(c + 2, 2 * nscan - 1), lga_ref)
        ms, ls = softmax_step(c + 1, lgb_ref, ms, ls, chunk_max(lgb_ref))
        return ms, ls, chunk_max(lga_ref)

    ms0 = tuple(jnp.full((1, 2 * Q), MASK_BIAS, F32) for _ in range(N_HEADS // 2))
    ls0 = tuple(jnp.zeros((1, 2 * Q), F32) for _ in range(N_HEADS // 2))
    _, ls, _ = lax.fori_loop(0, nscan, attn_body, (ms0, ls0, chunk_max(lga_ref)))
    outs = []
    for pr in range(N_HEADS // 2):
        a0 = pr * 128
        outs.append(acc_ref[a0:a0 + 64, :] / ls[pr][:, 0:Q])
        outs.append(acc_ref[a0 + 64:a0 + 128, :] / ls[pr][:, Q:2 * Q])
    o_ref[0] = jnp.concatenate(outs, axis=0).T * _silu(ag_ref[0])


def _attention(qt, qi, wt, ag, ws, k, vt, topk):
    B, S, _ = k.shape
    nq = S // Q_BLOCK
    nch = S // KEY_CHUNK
    qblk = lambda r, w: pl.BlockSpec((1, 1, r, w), lambda b, j: (b, j, 0, 0))
    once = pl.Buffered(1)
    seq = lambda w: pl.BlockSpec((1, S, w), lambda b, j: (b, 0, 0), pipeline_mode=once)
    return pl.pallas_call(
        functools.partial(_attn_kernel, topk=topk),
        grid=(B, nq),
        in_specs=[
            qblk(ATTN_W, Q_BLOCK), qblk(128, IDX_HEADS * Q_BLOCK), qblk(SUBLANES, Q_BLOCK),
            pl.BlockSpec((1, Q_BLOCK, ATTN_W), lambda b, j: (b, j, 0)),
            seq(128), seq(ATTN_W),
            pl.BlockSpec((1, nch, VT_ROWS, KEY_CHUNK), lambda b, j: (b, 0, 0, 0), pipeline_mode=once),
        ],
        out_specs=pl.BlockSpec((1, Q_BLOCK, ATTN_W), lambda b, j: (b, j, 0)),
        out_shape=jax.ShapeDtypeStruct((B, S, ATTN_W), F32),
        scratch_shapes=[
            pltpu.VMEM((S, Q_BLOCK), F32),
            pltpu.VMEM((N_HEADS // 2, 256, 2 * Q_BLOCK), BF16),
            pltpu.VMEM((ATTN_W, Q_BLOCK), F32),
            pltpu.VMEM((N_HEADS // 2, KEY_CHUNK, 2 * Q_BLOCK), F32),
            pltpu.VMEM((N_HEADS // 2, KEY_CHUNK, 2 * Q_BLOCK), F32),
            pltpu.VMEM((N_HEADS // 2, KEY_CHUNK, 2 * Q_BLOCK), BF16),
        ],
        compiler_params=pltpu.CompilerParams(
            dimension_semantics=("arbitrary", "arbitrary"), vmem_limit_bytes=VMEM_LIMIT),
        name="attention",
    )(qt, qi, wt, ag, ws, k, vt)


def _outproj_kernel(h_ref, yab_ref, yc_ref, p_ref, wo_ref, plew_ref, pleg_ref, fng_ref, o_ref,
                    *, final):
    h = (h_ref[0] + _dot(yab_ref[0].astype(BF16), wo_ref[0:512, :])
         + _dot(yc_ref[0].astype(BF16), wo_ref[512:1024, :]))
    ple = _dot(p_ref[0].astype(BF16), plew_ref[...])
    h = h + ple * _sigmoid(_dot(h.astype(BF16), pleg_ref[...]))
    if final:
        h = h * lax.rsqrt(jnp.mean(h * h, axis=-1, keepdims=True) + EPS) * fng_ref[...]
    o_ref[0] = h


def _outproj(h, yab, yc, p, wo, plew, pleg, fng, final):
    B, S, D = h.shape
    nt = S // ROW_TILE
    row = lambda w: pl.BlockSpec((1, ROW_TILE, w), lambda b, t: (b, t, 0))
    full = lambda a: pl.BlockSpec(a.shape, lambda b, t: (0,) * a.ndim)
    return pl.pallas_call(
        functools.partial(_outproj_kernel, final=final),
        grid=(B, nt),
        in_specs=[row(D), row(512), row(512), row(p.shape[-1]), full(wo), full(plew), full(pleg),
                  full(fng)],
        out_specs=row(D),
        out_shape=jax.ShapeDtypeStruct((B, S, D), F32),
        compiler_params=pltpu.CompilerParams(
            dimension_semantics=("arbitrary", "arbitrary"), vmem_limit_bytes=VMEM_LIMIT),
        name="outproj",
    )(h, yab, yc, p, wo, plew, pleg, fng)


def _block_diag(w):
    g, a, b = w.shape
    out = jnp.zeros((g * a, g * b), w.dtype)
    for i in range(g):
        out = out.at[i * a:(i + 1) * a, i * b:(i + 1) * b].set(w[i])
    return out


def kernel(x, p, norm_g, w_in, b_in, conv_dw_w, conv_dw_b, conv_ln_g, conv_ln_b, conv_pw_w,
           conv_pw_b, pool_w, pool_b, pool_scale, w_out, ple_w, ple_gate_w, final_norm_g):
    B, S, D = x.shape
    depth = w_in.shape[0]
    assert S % ROW_TILE == 0 and D == 1024 and w_in.shape[2] == N_MAIN + 324
    topk = min(TOPK_MAX, S // 4)

    half = HEAD_DIM // 2
    inv = ROPE_THETA ** (-jnp.arange(half, dtype=F32) / half)
    ang = jnp.arange(S).astype(F32)[:, None] * inv[None, :]
    cos = jnp.tile(jnp.cos(ang), (1, 4))
    sin = jnp.tile(jnp.concatenate([-jnp.sin(ang), jnp.sin(ang)], axis=1), (1, 2))

    row2 = lambda a: a.reshape(1, -1)
    h = x
    for i in range(depth):
        wm = w_in[i, :, :N_MAIN].astype(BF16)
        bm = row2(b_in[i, :N_MAIN])
        wx = jnp.pad(w_in[i, :, N_MAIN:], ((0, 0), (0, IDX_PAD - 324)))
        wxh = wx.astype(BF16)
        bx = row2(jnp.pad(b_in[i, N_MAIN:], (0, IDX_PAD - 324)))
        zc, zp, qt, k, vt, ag, qi, ws, wt = _inproj(
            h, row2(norm_g[i]), wm, bm, wxh, (wx - wxh.astype(F32)).astype(BF16), bx, cos, sin)
        yab = _convpool(zc, zp, conv_dw_w[i], row2(conv_dw_b[i]), row2(conv_ln_g[i]),
                        row2(conv_ln_b[i]), conv_pw_w[i].astype(BF16), row2(conv_pw_b[i]),
                        _block_diag(pool_w[i]).astype(BF16), row2(pool_b[i]), row2(pool_scale[i]))
        yc = _attention(qt, qi, wt, ag, ws, k, vt, topk)
        h = _outproj(h, yab, yc, p[i], w_out[i].astype(BF16), ple_w[i].astype(BF16),
                     ple_gate_w[i].astype(BF16), row2(final_norm_g), final=(i == depth - 1))
    return h
```

```python
import functools

import jax
import jax.numpy as jnp
from jax import lax
from jax.experimental import pallas as pl
from jax.experimental.pallas import tpu as pltpu

F32 = jnp.float32
BF16 = jnp.bfloat16

CONV_W = 256
CONV_K = 31
POOL_W = 256
POOL_GW = 64
POOL_WINDOWS = (2, 4, 8, 16)
HEAD_DIM = 64
ATTN_W = 512
N_HEADS = 8
IDX_HEADS = 4
IDX_DIM = 64
TOPK_MAX = 256
ROPE_THETA = 10000.0
EPS = 1e-6

SUBLANES = 8
Q_BLOCK = 128
KEY_CHUNK = 256
SCAN_ROWS = 2 * KEY_CHUNK
BIAS_UNROLL = 4
PAIR_ROWS = 128 + 16
VT_ROWS = (N_HEADS // 2) * PAIR_ROWS
FIRST_PASSES = 16
NEXT_PASSES = 2
INTERP_AFTER = 5
INTERP_CLIP = 0.25
LOG2E = 1.4426950408889634
HALO = 32
ROW_TILE = 512
IDX_PAD = 384
N_MAIN = 3 * CONV_W + 2 * POOL_W + 4 * ATTN_W
MASK_BIAS = -1e30
NEED_ALL = 1e9
VMEM_LIMIT = 60 * 1024 * 1024


def _sigmoid(x):
    return 1.0 / (1.0 + jnp.exp(-x))


def _silu(x):
    return x * _sigmoid(x)


def _dot(a, b):
    return lax.dot_general(a, b, (((1,), (0,)), ((), ())), preferred_element_type=F32)


def _rope_slab(x, cos, sin):
    lane = lax.broadcasted_iota(jnp.int32, x.shape, 1)
    swapped = jnp.where((lane & 32) == 0, pltpu.roll(x, 96, 1), pltpu.roll(x, 32, 1))
    return x * cos + swapped * sin


def _rope(x, cos, sin):
    return jnp.concatenate(
        [_rope_slab(x[:, s:s + 128], cos, sin) for s in range(0, x.shape[1], 128)], axis=1)


def _inproj_kernel(h_ref, g_ref, wm_ref, bm_ref, wxh_ref, wxl_ref, bx_ref, cos_ref, sin_ref,
                   zc_ref, zp_ref, qt_ref, k_ref, vt_ref, ag_ref, qi_ref, ws_ref, wt_ref):
    x = h_ref[0]
    n = x * lax.rsqrt(jnp.mean(x * x, axis=-1, keepdims=True) + EPS) * g_ref[...]
    nb = n.astype(BF16)
    cos = cos_ref[...]
    sin = sin_ref[...]

    def proj(a, b):
        return _dot(nb, wm_ref[:, a:b]) + bm_ref[:, a:b]

    zc_ref[0] = proj(0, 768)
    zp_ref[0] = proj(768, 1280)
    qt = (_rope(proj(1280, 1792), cos, sin) * (HEAD_DIM ** -0.5 * LOG2E)).T.astype(BF16)
    for jb in range(ROW_TILE // Q_BLOCK):
        qt_ref[0, jb] = qt[:, jb * Q_BLOCK:(jb + 1) * Q_BLOCK]
    k_ref[0] = _rope(proj(1792, 2304), cos, sin).astype(BF16)
    vt = proj(2304, 2816).T.astype(BF16)
    ones = jnp.ones((PAIR_ROWS - 128, KEY_CHUNK), BF16)
    for c in range(ROW_TILE // KEY_CHUNK):
        vt_ref[0, c] = jnp.concatenate(
            [piece for pr in range(N_HEADS // 2)
             for piece in (vt[pr * 128:(pr + 1) * 128, c * KEY_CHUNK:(c + 1) * KEY_CHUNK], ones)], axis=0)
    ag_ref[0] = proj(2816, 3328)

    nl = (n - nb.astype(F32)).astype(BF16)
    zx = (_dot(nb, wxh_ref[...]) + _dot(nb, wxl_ref[...]) + _dot(nl, wxh_ref[...])) + bx_ref[...]
    qit = (_rope(zx[:, 0:256], cos, sin) * (IDX_DIM ** -0.5)).T.astype(BF16)
    pad = jnp.zeros((128 - IDX_DIM, Q_BLOCK), BF16)
    for jb in range(ROW_TILE // Q_BLOCK):
        qi_ref[0, jb] = jnp.concatenate(
            [jnp.concatenate([qit[h * IDX_DIM:(h + 1) * IDX_DIM, jb * Q_BLOCK:(jb + 1) * Q_BLOCK],
                              pad], axis=0) for h in range(IDX_HEADS)], axis=1)
    tail = zx[:, 256:384]
    lane = lax.broadcasted_iota(jnp.int32, tail.shape, 1)
    tail = jnp.where(lane < IDX_DIM, _rope_slab(tail, cos, sin), tail)
    ws_ref[0] = tail.astype(BF16)
    wt = tail.T[IDX_DIM:IDX_DIM + SUBLANES, :]
    for jb in range(ROW_TILE // Q_BLOCK):
        wt_ref[0, jb] = wt[:, jb * Q_BLOCK:(jb + 1) * Q_BLOCK]


def _inproj(h, g, wm, bm, wxh, wxl, bx, cos, sin):
    B, S, D = h.shape
    nt = S // ROW_TILE
    cpt = ROW_TILE // KEY_CHUNK
    qpt = ROW_TILE // Q_BLOCK
    nch = S // KEY_CHUNK
    nq = S // Q_BLOCK
    row = lambda w: pl.BlockSpec((1, ROW_TILE, w), lambda b, t: (b, t, 0))
    full = lambda a: pl.BlockSpec(a.shape, lambda b, t: (0,) * a.ndim)
    tab = pl.BlockSpec((ROW_TILE, 128), lambda b, t: (t, 0))
    qblk = lambda r, w: pl.BlockSpec((1, qpt, r, w), lambda b, t: (b, t, 0, 0))
    out_shape = (
        jax.ShapeDtypeStruct((B, S, 768), F32),
        jax.ShapeDtypeStruct((B, S, 512), F32),
        jax.ShapeDtypeStruct((B, nq, ATTN_W, Q_BLOCK), BF16),
        jax.ShapeDtypeStruct((B, S, ATTN_W), BF16),
        jax.ShapeDtypeStruct((B, nch, VT_ROWS, KEY_CHUNK), BF16),
        jax.ShapeDtypeStruct((B, S, ATTN_W), F32),
        jax.ShapeDtypeStruct((B, nq, 128, IDX_HEADS * Q_BLOCK), BF16),
        jax.ShapeDtypeStruct((B, S, 128), BF16),
        jax.ShapeDtypeStruct((B, nq, SUBLANES, Q_BLOCK), F32),
    )
    out_specs = (
        row(768), row(512), qblk(ATTN_W, Q_BLOCK), row(ATTN_W),
        pl.BlockSpec((1, cpt, VT_ROWS, KEY_CHUNK), lambda b, t: (b, t, 0, 0)),
        row(ATTN_W), qblk(128, IDX_HEADS * Q_BLOCK), row(128), qblk(SUBLANES, Q_BLOCK),
    )
    return pl.pallas_call(
        _inproj_kernel,
        grid=(B, nt),
        in_specs=[row(D), full(g), full(wm), full(bm), full(wxh), full(wxl), full(bx), tab, tab],
        out_specs=out_specs,
        out_shape=out_shape,
        compiler_params=pltpu.CompilerParams(
            dimension_semantics=("arbitrary", "arbitrary"), vmem_limit_bytes=VMEM_LIMIT),
        name="inproj",
    )(h, g, wm, bm, wxh, wxl, bx, cos, sin)


def _convpool_kernel(zc_ref, zch_ref, zp_ref, zph_ref, dw_ref, dwb_ref, lng_ref, lnb_ref,
                     pw_ref, pwb_ref, plw_ref, plb_ref, pls_ref, o_ref, ubuf, xbuf, shift):
    t = pl.program_id(1)
    keep = (t > 0).astype(F32)

    zc = zc_ref[0]
    zch = zch_ref[0]
    ubuf[0:HALO] = zch[:, 0:256] * _sigmoid(zch[:, 256:512]) * keep
    ubuf[HALO:HALO + ROW_TILE] = zc[:, 0:256] * _sigmoid(zc[:, 256:512])
    def window(buf, off):
        r = off % SUBLANES
        return (buf[off:off + ROW_TILE, :] if r == 0
                else shift[r - 1, off - r:off - r + ROW_TILE, :])

    def make_shifts(buf):
        for r in range(1, SUBLANES):
            shift[r - 1] = buf[pl.ds(r, HALO + ROW_TILE - SUBLANES), :]

    make_shifts(ubuf)
    acc = jnp.zeros((ROW_TILE, CONV_W), F32) + dwb_ref[...]
    for j in range(CONV_K):
        acc = acc + window(ubuf, HALO - (CONV_K - 1) + j) * dw_ref[j:j + 1, :]
    mu = jnp.mean(acc, axis=-1, keepdims=True)
    xc = acc - mu
    y = xc * lax.rsqrt(jnp.mean(xc * xc, axis=-1, keepdims=True) + EPS) * lng_ref[...] + lnb_ref[...]
    y = _silu(y)
    ya = (_dot(y.astype(BF16), pw_ref[...]) + pwb_ref[...]) * _silu(zc[:, 512:768])

    zp = zp_ref[0]
    x = zp[:, 0:256]
    xbuf[0:HALO] = zph_ref[0][:, 0:256] * keep
    xbuf[HALO:HALO + ROW_TILE] = x
    make_shifts(xbuf)
    run = x
    sums = {}
    for d in range(1, POOL_WINDOWS[-1]):
        run = run + window(xbuf, HALO - d)
        if d + 1 in POOL_WINDOWS:
            sums[d + 1] = run
    lane = lax.broadcasted_iota(jnp.int32, (ROW_TILE, POOL_W), 1)
    grp = lane // POOL_GW
    wsum = jnp.where(grp == 0, sums[2], jnp.where(grp == 1, sums[4],
                                                  jnp.where(grp == 2, sums[8], sums[16])))
    win = jnp.where(grp == 0, 2, jnp.where(grp == 1, 4, jnp.where(grp == 2, 8, 16)))
    pos = t * ROW_TILE + lax.broadcasted_iota(jnp.int32, (ROW_TILE, POOL_W), 0)
    cnt = jnp.minimum(pos + 1, win).astype(F32)
    d = wsum / cnt - x
    yb = (_dot(d.astype(BF16), plw_ref[...]) + plb_ref[...]) * pls_ref[...] * _silu(zp[:, 256:512])

    o_ref[0] = jnp.concatenate([ya, yb], axis=1)


def _convpool(zc, zp, dw, dwb, lng, lnb, pw, pwb, plw, plb, pls):
    B, S, _ = zc.shape
    nt = S // ROW_TILE
    hpt = ROW_TILE // HALO
    row = lambda w: pl.BlockSpec((1, ROW_TILE, w), lambda b, t: (b, t, 0))
    halo = lambda w: pl.BlockSpec((1, HALO, w), lambda b, t: (b, jnp.maximum(t * hpt - 1, 0), 0))
    full = lambda a: pl.BlockSpec(a.shape, lambda b, t: (0,) * a.ndim)
    return pl.pallas_call(
        _convpool_kernel,
        grid=(B, nt),
        in_specs=[row(768), halo(768), row(512), halo(512), full(dw), full(dwb), full(lng),
                  full(lnb), full(pw), full(pwb), full(plw), full(plb), full(pls)],
        out_specs=row(512),
        out_shape=jax.ShapeDtypeStruct((B, S, 512), F32),
        scratch_shapes=[pltpu.VMEM((HALO + ROW_TILE, CONV_W), F32),
                        pltpu.VMEM((HALO + ROW_TILE, POOL_W), F32),
                        pltpu.VMEM((SUBLANES - 1, HALO + ROW_TILE - SUBLANES, CONV_W), F32)],
        compiler_params=pltpu.CompilerParams(
            dimension_semantics=("arbitrary", "arbitrary"), vmem_limit_bytes=VMEM_LIMIT),
        name="convpool",
    )(zc, zc, zp, zp, dw, dwb, lng, lnb, pw, pwb, plw, plb, pls)


def _fold_rows(x, op, fn=lambda p: p):
    return _combine([fn(x[r:r + SUBLANES, :]) for r in range(0, x.shape[0], SUBLANES)], op)


def _combine(parts, op):
    while len(parts) > 1:
        parts = [op(parts[i], parts[i + 1]) for i in range(0, len(parts) - 1, 2)] + (
            [parts[-1]] if len(parts) % 2 else [])
    return parts[0]


def _attn_kernel(qt_ref, qi_ref, wt_ref, ag_ref, ws_ref, k_ref, vt_ref, o_ref,
                 sc_ref, qp_ref, acc_ref, lga_ref, lgb_ref, p_ref, *, topk):
    j = pl.program_id(1)
    nfull = j // 2
    nch = nfull + 1
    Q = Q_BLOCK
    KC = KEY_CHUNK
    qpos = j * Q + lax.broadcasted_iota(jnp.int32, (1, Q), 1)

    qi = qi_ref[0, 0]
    wt = wt_ref[0, 0]
    w_h = [wt[h:h + 1, :] for h in range(IDX_HEADS)]

    def unit_scores(u):
        r0 = pl.multiple_of(u * SCAN_ROWS, SCAN_ROWS)
        s = _dot(ws_ref[0, pl.ds(r0, SCAN_ROWS), :], qi)
        tot = jnp.zeros((SCAN_ROWS, Q), F32)
        for h in range(IDX_HEADS):
            tot = tot + w_h[h] * jnp.maximum(s[:, h * Q:(h + 1) * Q], 0.0)
        return tot

    def score_body(i, carry):
        mx, mn = carry
        for u in (2 * i, 2 * i + 1):
            s = unit_scores(u)
            sc_ref[pl.ds(pl.multiple_of(u * SCAN_ROWS, SCAN_ROWS), SCAN_ROWS), :] = s
            mx = jnp.maximum(mx, _fold_rows(s, jnp.maximum))
            mn = jnp.minimum(mn, _fold_rows(s, jnp.minimum))
        return mx, mn

    nscan = (nch + 1) // 2
    nbias = (nch + BIAS_UNROLL - 1) // BIAS_UNROLL
    init = (jnp.full((SUBLANES, Q), -jnp.inf, F32), jnp.full((SUBLANES, Q), jnp.inf, F32))
    mx, mn = lax.fori_loop(0, nbias, score_body, init)
    rowmax = jnp.max(mx, axis=0, keepdims=True)
    rowmin = jnp.min(mn, axis=0, keepdims=True)

    d0 = pl.multiple_of(nfull * KC, KC)
    kpos = nfull * KC + lax.broadcasted_iota(jnp.int32, (KC, Q), 0)
    sc_ref[pl.ds(d0, KC), :] = jnp.where(kpos <= qpos, sc_ref[pl.ds(d0, KC), :], -jnp.inf)

    def pad_body(c, _):
        sc_ref[pl.ds(pl.multiple_of(c * KC, KC), KC), :] = jnp.full((KC, Q), -jnp.inf, F32)
        return 0

    lax.fori_loop(nch, nbias * BIAS_UNROLL, pad_body, 0)

    ncausal = (qpos + 1).astype(F32)
    kk = jnp.minimum(ncausal, float(topk))

    def count_ge(t):
        tb = jnp.broadcast_to(t, (SUBLANES, Q))

        def body(c, acc):
            for u in (2 * c, 2 * c + 1):
                sv = sc_ref[pl.ds(pl.multiple_of(u * SCAN_ROWS, SCAN_ROWS), SCAN_ROWS), :]
                acc = acc + _fold_rows(sv, jnp.add, lambda p: jnp.where(p >= tb, 1.0, 0.0))
            return acc

        acc = lax.fori_loop(0, nbias, body, jnp.zeros((SUBLANES, Q), F32))
        return jnp.sum(acc, axis=0, keepdims=True)

    def bracket_ends(lo, hi):
        lob = jnp.broadcast_to(lo, (SUBLANES, Q))
        hib = jnp.broadcast_to(hi, (SUBLANES, Q))

        def body(c, carry):
            a, b = carry
            sv = sc_ref[pl.ds(pl.multiple_of(c * SCAN_ROWS, SCAN_ROWS), SCAN_ROWS), :]
            a = jnp.minimum(a, _fold_rows(sv, jnp.minimum, lambda p: jnp.where(p >= lob, p, jnp.inf)))
            b = jnp.maximum(b, _fold_rows(sv, jnp.maximum, lambda p: jnp.where(p < hib, p, -jnp.inf)))
            return a, b

        a, b = lax.fori_loop(0, nscan, body, (jnp.full((SUBLANES, Q), jnp.inf, F32),
                                            jnp.full((SUBLANES, Q), -jnp.inf, F32)))
        return jnp.min(a, axis=0, keepdims=True), jnp.max(b, axis=0, keepdims=True)

    res0 = (ncausal <= kk).astype(F32)
    state0 = dict(lo=rowmin, hi=jnp.full((1, Q), jnp.inf, F32), c_lo=ncausal,
                  c_hi=jnp.zeros((1, Q), F32), res=res0, thr=rowmin,
                  need=jnp.full((1, Q), NEED_ALL, F32), pivot=rowmax,
                  use_pivot=jnp.ones((1, Q), F32), npass=jnp.zeros((1, Q), F32))

    def bisect_pass(_, st):
        lo, hi = st["lo"], st["hi"]
        bounded = hi < jnp.inf
        frac = jnp.clip((st["c_lo"] - kk + 0.5) / jnp.maximum(st["c_lo"] - st["c_hi"], 1.0),
                        INTERP_CLIP, 1.0 - INTERP_CLIP)
        frac = jnp.where(st["npass"] >= INTERP_AFTER, frac, 0.5)
        mid = jnp.where(st["use_pivot"] > 0, st["pivot"],
                        jnp.where(bounded, lo + (hi - lo) * frac, lo))
        cnt = count_ge(mid)
        live = st["res"] == 0
        ge = cnt >= kk
        hit = live & (cnt == kk)
        up = live & ge
        dn = live & jnp.logical_not(ge)
        return dict(
            lo=jnp.where(up, mid, lo), c_lo=jnp.where(up, cnt, st["c_lo"]),
            hi=jnp.where(dn, mid, hi), c_hi=jnp.where(dn, cnt, st["c_hi"]),
            res=jnp.where(hit, 1.0, st["res"]), thr=jnp.where(hit, mid, st["thr"]),
            need=st["need"], pivot=st["pivot"], use_pivot=jnp.zeros((1, Q), F32),
            npass=st["npass"] + 1.0)

    def round_body(carry):
        it, st = carry
        st = lax.fori_loop(0, jnp.where(it == 0, FIRST_PASSES, NEXT_PASSES), bisect_pass, st)
        a, b = bracket_ends(st["lo"], st["hi"])
        live = st["res"] == 0
        tie = live & (a == b)
        st = dict(st)
        last = live & (kk - st["c_hi"] == 1.0)
        st["thr"] = jnp.where(tie, a, jnp.where(last, b, st["thr"]))
        st["need"] = jnp.where(tie | last, kk - st["c_hi"], st["need"])
        st["res"] = jnp.where(tie | last, 1.0, st["res"])
        st["lo"] = jnp.where(live, a, st["lo"])
        st["pivot"] = b
        st["use_pivot"] = jnp.ones((1, Q), F32)
        return it + 1, st

    def round_cond(carry):
        _, st = carry
        return jnp.min(st["res"]) == 0

    _, st = lax.while_loop(round_cond, round_body, (jnp.int32(0), state0))
    thr = st["thr"]
    need0 = st["need"]

    qt = qt_ref[0, 0]
    rowhead = lax.broadcasted_iota(jnp.int32, (256, Q), 0) // HEAD_DIM
    zero = jnp.zeros((), BF16)
    for pr in range(N_HEADS // 2):
        qg = qt[(pr // 2) * 256:(pr // 2 + 1) * 256, :]
        qp_ref[pr] = jnp.concatenate([jnp.where(rowhead == (2 * pr) % 4, qg, zero),
                                      jnp.where(rowhead == (2 * pr + 1) % 4, qg, zero)], axis=1)
    acc_ref[...] = jnp.zeros(acc_ref.shape, F32)
    ri = lax.broadcasted_iota(jnp.int32, (KC, KC), 0)
    ci = lax.broadcasted_iota(jnp.int32, (KC, KC), 1)
    tril = jnp.where(ci <= ri, 1.0, 0.0).astype(BF16)

    def bias_body(u, need):
        for i in range(BIAS_UNROLL):
            r0 = pl.multiple_of((u * BIAS_UNROLL + i) * KC, KC)
            sv = sc_ref[pl.ds(r0, KC), :]
            eq = sv == thr
            tie = jnp.where(eq, 1.0, 0.0)
            prefix = _dot(tril, tie.astype(BF16))
            take_tie = jnp.where(prefix <= need, 0.0, MASK_BIAS)
            sc_ref[pl.ds(r0, KC), :] = jnp.where(sv > thr, 0.0, jnp.where(eq, take_tie, MASK_BIAS))
            need = need - jnp.sum(_fold_rows(tie, jnp.add), axis=0, keepdims=True)
        return need

    lax.fori_loop(0, nbias, bias_body, need0)

    def logits_into(c, lg_ref):
        r0 = pl.multiple_of(c * KC, KC)
        bias = sc_ref[pl.ds(r0, KC), :]
        bias2 = jnp.concatenate([bias, bias], axis=1)
        for pr in range(N_HEADS // 2):
            g0 = (pr // 2) * 256
            lg_ref[pr] = _dot(k_ref[0, pl.ds(r0, KC), g0:g0 + 256], qp_ref[pr]) + bias2

    def chunk_max(lg_ref):
        return tuple(jnp.max(_fold_rows(lg_ref[pr], jnp.maximum), axis=0, keepdims=True)
                     for pr in range(N_HEADS // 2))

    def softmax_step(c, lg_ref, ms, ls, cmax):
        new_ms, new_ls = [], []
        for pr in range(N_HEADS // 2):
            m_new = jnp.maximum(ms[pr], cmax[pr])
            alpha = jnp.exp2(ms[pr] - m_new)
            for r in range(0, KC, 2 * SUBLANES):
                pe = jnp.exp2(lg_ref[pr, r:r + 2 * SUBLANES, :] - m_new)
                p_ref[pr, r:r + 2 * SUBLANES, :] = pe.astype(BF16)
            o = _dot(vt_ref[0, c, pr * PAIR_ROWS:(pr + 1) * PAIR_ROWS, :], p_ref[pr])
            new_ls.append(alpha * ls[pr] + o[128:129, :])
            new_ms.append(m_new)
            a0 = pr * 128
            acc_ref[a0:a0 + 64, :] = acc_ref[a0:a0 + 64, :] * alpha[:, 0:Q] + o[0:64, 0:Q]
            acc_ref[a0 + 64:a0 + 128, :] = (acc_ref[a0 + 64:a0 + 128, :] * alpha[:, Q:2 * Q]
                                            + o[64:128, Q:2 * Q])
        return tuple(new_ms), tuple(new_ls)

    logits_into(0, lga_ref)

    def attn_body(i, carry):
        ms, ls, cmax = carry
        c = 2 * i
        logits_into(c + 1, lgb_ref)
        ms, ls = softmax_step(c, lga_ref, ms, ls, cmax)
        logits_into(jnp.minimum(c + 2, 2 * nscan - 1), lga_ref)
        ms, ls = softmax_step(c + 1, lgb_ref, ms, ls, chunk_max(lgb_ref))
        return ms, ls, chunk_max(lga_ref)

    ms0 = tuple(jnp.full((1, 2 * Q), MASK_BIAS, F32) for _ in range(N_HEADS // 2))
    ls0 = tuple(jnp.zeros((1, 2 * Q), F32) for _ in range(N_HEADS // 2))
    _, ls, _ = lax.fori_loop(0, nscan, attn_body, (ms0, ls0, chunk_max(lga_ref)))
    outs = []
    for pr in range(N_HEADS // 2):
        a0 = pr * 128
        outs.append(acc_ref[a0:a0 + 64, :] / ls[pr][:, 0:Q])
        outs.append(acc_ref[a0 + 64:a0 + 128, :] / ls[pr][:, Q:2 * Q])
    o_ref[0] = jnp.concatenate(outs, axis=0).T * _silu(ag_ref[0])


def _attention(qt, qi, wt, ag, ws, k, vt, topk):
    B, S, _ = k.shape
    nq = S // Q_BLOCK
    nch = S // KEY_CHUNK
    qblk = lambda r, w: pl.BlockSpec((1, 1, r, w), lambda b, j: (b, j, 0, 0))
    once = pl.Buffered(1)
    seq = lambda w: pl.BlockSpec((1, S, w), lambda b, j: (b, 0, 0), pipeline_mode=once)
    return pl.pallas_call(
        functools.partial(_attn_kernel, topk=topk),
        grid=(B, nq),
        in_specs=[
            qblk(ATTN_W, Q_BLOCK), qblk(128, IDX_HEADS * Q_BLOCK), qblk(SUBLANES, Q_BLOCK),
            pl.BlockSpec((1, Q_BLOCK, ATTN_W), lambda b, j: (b, j, 0)),
            seq(128), seq(ATTN_W),
            pl.BlockSpec((1, nch, VT_ROWS, KEY_CHUNK), lambda b, j: (b, 0, 0, 0), pipeline_mode=once),
        ],
        out_specs=pl.BlockSpec((1, Q_BLOCK, ATTN_W), lambda b, j: (b, j, 0)),
        out_shape=jax.ShapeDtypeStruct((B, S, ATTN_W), F32),
        scratch_shapes=[
            pltpu.VMEM((S, Q_BLOCK), F32),
            pltpu.VMEM((N_HEADS // 2, 256, 2 * Q_BLOCK), BF16),
            pltpu.VMEM((ATTN_W, Q_BLOCK), F32),
            pltpu.VMEM((N_HEADS // 2, KEY_CHUNK, 2 * Q_BLOCK), F32),
            pltpu.VMEM((N_HEADS // 2, KEY_CHUNK, 2 * Q_BLOCK), F32),
            pltpu.VMEM((N_HEADS // 2, KEY_CHUNK, 2 * Q_BLOCK), BF16),
        ],
        compiler_params=pltpu.CompilerParams(
            dimension_semantics=("arbitrary", "arbitrary"), vmem_limit_bytes=VMEM_LIMIT),
        name="attention",
    )(qt, qi, wt, ag, ws, k, vt)


def _outproj_kernel(h_ref, yab_ref, yc_ref, p_ref, wo_ref, plew_ref, pleg_ref, fng_ref, o_ref,
                    *, final):
    h = (h_ref[0] + _dot(yab_ref[0].astype(BF16), wo_ref[0:512, :])
         + _dot(yc_ref[0].astype(BF16), wo_ref[512:1024, :]))
    ple = _dot(p_ref[0].astype(BF16), plew_ref[...])
    h = h + ple * _sigmoid(_dot(h.astype(BF16), pleg_ref[...]))
    if final:
        h = h * lax.rsqrt(jnp.mean(h * h, axis=-1, keepdims=True) + EPS) * fng_ref[...]
    o_ref[0] = h


def _outproj(h, yab, yc, p, wo, plew, pleg, fng, final):
    B, S, D = h.shape
    nt = S // ROW_TILE
    row = lambda w: pl.BlockSpec((1, ROW_TILE, w), lambda b, t: (b, t, 0))
    full = lambda a: pl.BlockSpec(a.shape, lambda b, t: (0,) * a.ndim)
    return pl.pallas_call(
        functools.partial(_outproj_kernel, final=final),
        grid=(B, nt),
        in_specs=[row(D), row(512), row(512), row(p.shape[-1]), full(wo), full(plew), full(pleg),
                  full(fng)],
        out_specs=row(D),
        out_shape=jax.ShapeDtypeStruct((B, S, D), F32),
        compiler_params=pltpu.CompilerParams(
            dimension_semantics=("arbitrary", "arbitrary"), vmem_limit_bytes=VMEM_LIMIT),
        name="outproj",
    )(h, yab, yc, p, wo, plew, pleg, fng)


def _block_diag(w):
    g, a, b = w.shape
    out = jnp.zeros((g * a, g * b), w.dtype)
    for i in range(g):
        out = out.at[i * a:(i + 1) * a, i * b:(i + 1) * b].set(w[i])
    return out


def kernel(x, p, norm_g, w_in, b_in, conv_dw_w, conv_dw_b, conv_ln_g, conv_ln_b, conv_pw_w,
           conv_pw_b, pool_w, pool_b, pool_scale, w_out, ple_w, ple_gate_w, final_norm_g):
    B, S, D = x.shape
    depth = w_in.shape[0]
    assert S % ROW_TILE == 0 and D == 1024 and w_in.shape[2] == N_MAIN + 324
    topk = min(TOPK_MAX, S // 4)

    half = HEAD_DIM // 2
    inv = ROPE_THETA ** (-jnp.arange(half, dtype=F32) / half)
    ang = jnp.arange(S).astype(F32)[:, None] * inv[None, :]
    cos = jnp.tile(jnp.cos(ang), (1, 4))
    sin = jnp.tile(jnp.concatenate([-jnp.sin(ang), jnp.sin(ang)], axis=1), (1, 2))

    row2 = lambda a: a.reshape(1, -1)
    h = x
    for i in range(depth):
        wm = w_in[i, :, :N_MAIN].astype(BF16)
        bm = row2(b_in[i, :N_MAIN])
        wx = jnp.pad(w_in[i, :, N_MAIN:], ((0, 0), (0, IDX_PAD - 324)))
        wxh = wx.astype(BF16)
        bx = row2(jnp.pad(b_in[i, N_MAIN:], (0, IDX_PAD - 324)))
        zc, zp, qt, k, vt, ag, qi, ws, wt = _inproj(
            h, row2(norm_g[i]), wm, bm, wxh, (wx - wxh.astype(F32)).astype(BF16), bx, cos, sin)
        yab = _convpool(zc, zp, conv_dw_w[i], row2(conv_dw_b[i]), row2(conv_ln_g[i]),
                        row2(conv_ln_b[i]), conv_pw_w[i].astype(BF16), row2(conv_pw_b[i]),
                        _block_diag(pool_w[i]).astype(BF16), row2(pool_b[i]), row2(pool_scale[i]))
        yc = _attention(qt, qi, wt, ag, ws, k, vt, topk)
        h = _outproj(h, yab, yc, p[i], w_out[i].astype(BF16), ple_w[i].astype(BF16),
                     ple_gate_w[i].astype(BF16), row2(final_norm_g), final=(i == depth - 1))
    return h
```

```python
import functools

import jax
import jax.numpy as jnp
from jax import lax
from jax.experimental import pallas as pl
from jax.experimental.pallas import tpu as pltpu

F32 = jnp.float32
BF16 = jnp.bfloat16

CONV_W = 256
CONV_K = 31
POOL_W = 256
POOL_GW = 64
POOL_WINDOWS = (2, 4, 8, 16)
HEAD_DIM = 64
ATTN_W = 512
N_HEADS = 8
IDX_HEADS = 4
IDX_DIM = 64
TOPK_MAX = 256
ROPE_THETA = 10000.0
EPS = 1e-6

SUBLANES = 8
Q_BLOCK = 128
KEY_CHUNK = 256
SCAN_ROWS = 2 * KEY_CHUNK
BIAS_UNROLL = 2 * SCAN_ROWS // KEY_CHUNK
PAIR_ROWS = 128 + 16
VT_ROWS = (N_HEADS // 2) * PAIR_ROWS
FIRST_PASSES = 16
NEXT_PASSES = 2
INTERP_AFTER = 5
INTERP_CLIP = 0.25
LOG2E = 1.4426950408889634
HALO = 32
ROW_TILE = 512
IDX_PAD = 384
N_MAIN = 3 * CONV_W + 2 * POOL_W + 4 * ATTN_W
MASK_BIAS = -1e30
NEED_ALL = 1e9
VMEM_LIMIT = 60 * 1024 * 1024


def _sigmoid(x):
    return 1.0 / (1.0 + jnp.exp(-x))


def _silu(x):
    return x * _sigmoid(x)


def _dot(a, b):
    return lax.dot_general(a, b, (((1,), (0,)), ((), ())), preferred_element_type=F32)


def _rope_slab(x, cos, sin):
    lane = lax.broadcasted_iota(jnp.int32, x.shape, 1)
    swapped = jnp.where((lane & 32) == 0, pltpu.roll(x, 96, 1), pltpu.roll(x, 32, 1))
    return x * cos + swapped * sin


def _rope(x, cos, sin):
    return jnp.concatenate(
        [_rope_slab(x[:, s:s + 128], cos, sin) for s in range(0, x.shape[1], 128)], axis=1)


def _inproj_kernel(h_ref, g_ref, wm_ref, bm_ref, wxh_ref, wxl_ref, bx_ref, cos_ref, sin_ref,
                   zc_ref, zp_ref, qt_ref, k_ref, vt_ref, ag_ref, qi_ref, ws_ref, wt_ref):
    x = h_ref[0]
    n = x * lax.rsqrt(jnp.mean(x * x, axis=-1, keepdims=True) + EPS) * g_ref[...]
    nb = n.astype(BF16)
    cos = cos_ref[...]
    sin = sin_ref[...]

    def proj(a, b):
        return _dot(nb, wm_ref[:, a:b]) + bm_ref[:, a:b]

    zc_ref[0] = proj(0, 768)
    zp_ref[0] = proj(768, 1280)
    qt = (_rope(proj(1280, 1792), cos, sin) * (HEAD_DIM ** -0.5 * LOG2E)).T.astype(BF16)
    for jb in range(ROW_TILE // Q_BLOCK):
        qt_ref[0, jb] = qt[:, jb * Q_BLOCK:(jb + 1) * Q_BLOCK]
    k_ref[0] = _rope(proj(1792, 2304), cos, sin).astype(BF16)
    vt = proj(2304, 2816).T.astype(BF16)
    ones = jnp.ones((PAIR_ROWS - 128, KEY_CHUNK), BF16)
    for c in range(ROW_TILE // KEY_CHUNK):
        vt_ref[0, c] = jnp.concatenate(
            [piece for pr in range(N_HEADS // 2)
             for piece in (vt[pr * 128:(pr + 1) * 128, c * KEY_CHUNK:(c + 1) * KEY_CHUNK], ones)], axis=0)
    ag_ref[0] = proj(2816, 3328)

    nl = (n - nb.astype(F32)).astype(BF16)
    zx = (_dot(nb, wxh_ref[...]) + _dot(nb, wxl_ref[...]) + _dot(nl, wxh_ref[...])) + bx_ref[...]
    qit = (_rope(zx[:, 0:256], cos, sin) * (IDX_DIM ** -0.5)).T.astype(BF16)
    pad = jnp.zeros((128 - IDX_DIM, Q_BLOCK), BF16)
    for jb in range(ROW_TILE // Q_BLOCK):
        qi_ref[0, jb] = jnp.concatenate(
            [jnp.concatenate([qit[h * IDX_DIM:(h + 1) * IDX_DIM, jb * Q_BLOCK:(jb + 1) * Q_BLOCK],
                              pad], axis=0) for h in range(IDX_HEADS)], axis=1)
    tail = zx[:, 256:384]
    lane = lax.broadcasted_iota(jnp.int32, tail.shape, 1)
    tail = jnp.where(lane < IDX_DIM, _rope_slab(tail, cos, sin), tail)
    ws_ref[0] = tail.astype(BF16)
    wt = tail.T[IDX_DIM:IDX_DIM + SUBLANES, :]
    for jb in range(ROW_TILE // Q_BLOCK):
        wt_ref[0, jb] = wt[:, jb * Q_BLOCK:(jb + 1) * Q_BLOCK]


def _inproj(h, g, wm, bm, wxh, wxl, bx, cos, sin):
    B, S, D = h.shape
    nt = S // ROW_TILE
    cpt = ROW_TILE // KEY_CHUNK
    qpt = ROW_TILE // Q_BLOCK
    nch = S // KEY_CHUNK
    nq = S // Q_BLOCK
    row = lambda w: pl.BlockSpec((1, ROW_TILE, w), lambda b, t: (b, t, 0))
    full = lambda a: pl.BlockSpec(a.shape, lambda b, t: (0,) * a.ndim)
    tab = pl.BlockSpec((ROW_TILE, 128), lambda b, t: (t, 0))
    qblk = lambda r, w: pl.BlockSpec((1, qpt, r, w), lambda b, t: (b, t, 0, 0))
    out_shape = (
        jax.ShapeDtypeStruct((B, S, 768), F32),
        jax.ShapeDtypeStruct((B, S, 512), F32),
        jax.ShapeDtypeStruct((B, nq, ATTN_W, Q_BLOCK), BF16),
        jax.ShapeDtypeStruct((B, S, ATTN_W), BF16),
        jax.ShapeDtypeStruct((B, nch, VT_ROWS, KEY_CHUNK), BF16),
        jax.ShapeDtypeStruct((B, S, ATTN_W), F32),
        jax.ShapeDtypeStruct((B, nq, 128, IDX_HEADS * Q_BLOCK), BF16),
        jax.ShapeDtypeStruct((B, S, 128), BF16),
        jax.ShapeDtypeStruct((B, nq, SUBLANES, Q_BLOCK), F32),
    )
    out_specs = (
        row(768), row(512), qblk(ATTN_W, Q_BLOCK), row(ATTN_W),
        pl.BlockSpec((1, cpt, VT_ROWS, KEY_CHUNK), lambda b, t: (b, t, 0, 0)),
        row(ATTN_W), qblk(128, IDX_HEADS * Q_BLOCK), row(128), qblk(SUBLANES, Q_BLOCK),
    )
    return pl.pallas_call(
        _inproj_kernel,
        grid=(B, nt),
        in_specs=[row(D), full(g), full(wm), full(bm), full(wxh), full(wxl), full(bx), tab, tab],
        out_specs=out_specs,
        out_shape=out_shape,
        compiler_params=pltpu.CompilerParams(
            dimension_semantics=("arbitrary", "arbitrary"), vmem_limit_bytes=VMEM_LIMIT),
        name="inproj",
    )(h, g, wm, bm, wxh, wxl, bx, cos, sin)


def _convpool_kernel(zc_ref, zch_ref, zp_ref, zph_ref, dw_ref, dwb_ref, lng_ref, lnb_ref,
                     pw_ref, pwb_ref, plw_ref, plb_ref, pls_ref, o_ref, ubuf, xbuf, shift):
    t = pl.program_id(1)
    keep = (t > 0).astype(F32)

    zc = zc_ref[0]
    zch = zch_ref[0]
    ubuf[0:HALO] = zch[:, 0:256] * _sigmoid(zch[:, 256:512]) * keep
    ubuf[HALO:HALO + ROW_TILE] = zc[:, 0:256] * _sigmoid(zc[:, 256:512])
    def window(buf, off):
        r = off % SUBLANES
        return (buf[off:off + ROW_TILE, :] if r == 0
                else shift[r - 1, off - r:off - r + ROW_TILE, :])

    def make_shifts(buf):
        for r in range(1, SUBLANES):
            shift[r - 1] = buf[pl.ds(r, HALO + ROW_TILE - SUBLANES), :]

    make_shifts(ubuf)
    acc = jnp.zeros((ROW_TILE, CONV_W), F32) + dwb_ref[...]
    for j in range(CONV_K):
        acc = acc + window(ubuf, HALO - (CONV_K - 1) + j) * dw_ref[j:j + 1, :]
    mu = jnp.mean(acc, axis=-1, keepdims=True)
    xc = acc - mu
    y = xc * lax.rsqrt(jnp.mean(xc * xc, axis=-1, keepdims=True) + EPS) * lng_ref[...] + lnb_ref[...]
    y = _silu(y)
    ya = (_dot(y.astype(BF16), pw_ref[...]) + pwb_ref[...]) * _silu(zc[:, 512:768])

    zp = zp_ref[0]
    x = zp[:, 0:256]
    xbuf[0:HALO] = zph_ref[0][:, 0:256] * keep
    xbuf[HALO:HALO + ROW_TILE] = x
    make_shifts(xbuf)
    run = x
    sums = {}
    for d in range(1, POOL_WINDOWS[-1]):
        run = run + window(xbuf, HALO - d)
        if d + 1 in POOL_WINDOWS:
            sums[d + 1] = run
    lane = lax.broadcasted_iota(jnp.int32, (ROW_TILE, POOL_W), 1)
    grp = lane // POOL_GW
    wsum = jnp.where(grp == 0, sums[2], jnp.where(grp == 1, sums[4],
                                                  jnp.where(grp == 2, sums[8], sums[16])))
    win = jnp.where(grp == 0, 2, jnp.where(grp == 1, 4, jnp.where(grp == 2, 8, 16)))
    pos = t * ROW_TILE + lax.broadcasted_iota(jnp.int32, (ROW_TILE, POOL_W), 0)
    cnt = jnp.minimum(pos + 1, win).astype(F32)
    d = wsum / cnt - x
    yb = (_dot(d.astype(BF16), plw_ref[...]) + plb_ref[...]) * pls_ref[...] * _silu(zp[:, 256:512])

    o_ref[0] = jnp.concatenate([ya, yb], axis=1)


def _convpool(zc, zp, dw, dwb, lng, lnb, pw, pwb, plw, plb, pls):
    B, S, _ = zc.shape
    nt = S // ROW_TILE
    hpt = ROW_TILE // HALO
    row = lambda w: pl.BlockSpec((1, ROW_TILE, w), lambda b, t: (b, t, 0))
    halo = lambda w: pl.BlockSpec((1, HALO, w), lambda b, t: (b, jnp.maximum(t * hpt - 1, 0), 0))
    full = lambda a: pl.BlockSpec(a.shape, lambda b, t: (0,) * a.ndim)
    return pl.pallas_call(
        _convpool_kernel,
        grid=(B, nt),
        in_specs=[row(768), halo(768), row(512), halo(512), full(dw), full(dwb), full(lng),
                  full(lnb), full(pw), full(pwb), full(plw), full(plb), full(pls)],
        out_specs=row(512),
        out_shape=jax.ShapeDtypeStruct((B, S, 512), F32),
        scratch_shapes=[pltpu.VMEM((HALO + ROW_TILE, CONV_W), F32),
                        pltpu.VMEM((HALO + ROW_TILE, POOL_W), F32),
                        pltpu.VMEM((SUBLANES - 1, HALO + ROW_TILE - SUBLANES, CONV_W), F32)],
        compiler_params=pltpu.CompilerParams(
            dimension_semantics=("arbitrary", "arbitrary"), vmem_limit_bytes=VMEM_LIMIT),
        name="convpool",
    )(zc, zc, zp, zp, dw, dwb, lng, lnb, pw, pwb, plw, plb, pls)


def _fold_rows(x, op, fn=lambda p: p):
    return _combine([fn(x[r:r + SUBLANES, :]) for r in range(0, x.shape[0], SUBLANES)], op)


def _combine(parts, op):
    while len(parts) > 1:
        parts = [op(parts[i], parts[i + 1]) for i in range(0, len(parts) - 1, 2)] + (
            [parts[-1]] if len(parts) % 2 else [])
    return parts[0]


def _attn_kernel(qt_ref, qi_ref, wt_ref, ag_ref, ws_ref, k_ref, vt_ref, o_ref,
                 sc_ref, qp_ref, acc_ref, lga_ref, lgb_ref, p_ref, *, topk):
    j = pl.program_id(1)
    nfull = j // 2
    nch = nfull + 1
    Q = Q_BLOCK
    KC = KEY_CHUNK
    qpos = j * Q + lax.broadcasted_iota(jnp.int32, (1, Q), 1)

    qi = qi_ref[0, 0]
    wt = wt_ref[0, 0]
    w_h = [wt[h:h + 1, :] for h in range(IDX_HEADS)]

    def unit_scores(u):
        r0 = pl.multiple_of(u * SCAN_ROWS, SCAN_ROWS)
        s = _dot(ws_ref[0, pl.ds(r0, SCAN_ROWS), :], qi)
        tot = jnp.zeros((SCAN_ROWS, Q), F32)
        for h in range(IDX_HEADS):
            tot = tot + w_h[h] * jnp.maximum(s[:, h * Q:(h + 1) * Q], 0.0)
        return tot

    def score_body(i, carry):
        mx, mn = carry
        for u in (2 * i, 2 * i + 1):
            s = unit_scores(u)
            sc_ref[pl.ds(pl.multiple_of(u * SCAN_ROWS, SCAN_ROWS), SCAN_ROWS), :] = s
            mx = jnp.maximum(mx, _fold_rows(s, jnp.maximum))
            mn = jnp.minimum(mn, _fold_rows(s, jnp.minimum))
        return mx, mn

    nscan = (nch + 1) // 2
    nbias = (nch + BIAS_UNROLL - 1) // BIAS_UNROLL
    init = (jnp.full((SUBLANES, Q), -jnp.inf, F32), jnp.full((SUBLANES, Q), jnp.inf, F32))
    mx, mn = lax.fori_loop(0, nbias, score_body, init)
    rowmax = jnp.max(mx, axis=0, keepdims=True)
    rowmin = jnp.min(mn, axis=0, keepdims=True)

    d0 = pl.multiple_of(nfull * KC, KC)
    kpos = nfull * KC + lax.broadcasted_iota(jnp.int32, (KC, Q), 0)
    sc_ref[pl.ds(d0, KC), :] = jnp.where(kpos <= qpos, sc_ref[pl.ds(d0, KC), :], -jnp.inf)

    def pad_body(c, _):
        sc_ref[pl.ds(pl.multiple_of(c * KC, KC), KC), :] = jnp.full((KC, Q), -jnp.inf, F32)
        return 0

    lax.fori_loop(nch, nbias * BIAS_UNROLL, pad_body, 0)

    ncausal = (qpos + 1).astype(F32)
    kk = jnp.minimum(ncausal, float(topk))

    def count_ge(t):
        tb = jnp.broadcast_to(t, (SUBLANES, Q))

        def body(c, acc):
            for u in (2 * c, 2 * c + 1):
                sv = sc_ref[pl.ds(pl.multiple_of(u * SCAN_ROWS, SCAN_ROWS), SCAN_ROWS), :]
                acc = acc + _fold_rows(sv, jnp.add, lambda p: jnp.where(p >= tb, 1.0, 0.0))
            return acc

        acc = lax.fori_loop(0, nbias, body, jnp.zeros((SUBLANES, Q), F32))
        return jnp.sum(acc, axis=0, keepdims=True)

    def bracket_ends(lo, hi):
        lob = jnp.broadcast_to(lo, (SUBLANES, Q))
        hib = jnp.broadcast_to(hi, (SUBLANES, Q))

        def body(c, carry):
            a, b = carry
            sv = sc_ref[pl.ds(pl.multiple_of(c * SCAN_ROWS, SCAN_ROWS), SCAN_ROWS), :]
            a = jnp.minimum(a, _fold_rows(sv, jnp.minimum, lambda p: jnp.where(p >= lob, p, jnp.inf)))
            b = jnp.maximum(b, _fold_rows(sv, jnp.maximum, lambda p: jnp.where(p < hib, p, -jnp.inf)))
            return a, b

        a, b = lax.fori_loop(0, nscan, body, (jnp.full((SUBLANES, Q), jnp.inf, F32),
                                            jnp.full((SUBLANES, Q), -jnp.inf, F32)))
        return jnp.min(a, axis=0, keepdims=True), jnp.max(b, axis=0, keepdims=True)

    res0 = (ncausal <= kk).astype(F32)
    state0 = dict(lo=rowmin, hi=jnp.full((1, Q), jnp.inf, F32), c_lo=ncausal,
                  c_hi=jnp.zeros((1, Q), F32), res=res0, thr=rowmin,
                  need=jnp.full((1, Q), NEED_ALL, F32), pivot=rowmax,
                  use_pivot=jnp.ones((1, Q), F32), npass=jnp.zeros((1, Q), F32))

    def bisect_pass(_, st):
        lo, hi = st["lo"], st["hi"]
        bounded = hi < jnp.inf
        frac = jnp.clip((st["c_lo"] - kk + 0.5) / jnp.maximum(st["c_lo"] - st["c_hi"], 1.0),
                        INTERP_CLIP, 1.0 - INTERP_CLIP)
        frac = jnp.where(st["npass"] >= INTERP_AFTER, frac, 0.5)
        mid = jnp.where(st["use_pivot"] > 0, st["pivot"],
                        jnp.where(bounded, lo + (hi - lo) * frac, lo))
        cnt = count_ge(mid)
        live = st["res"] == 0
        ge = cnt >= kk
        hit = live & (cnt == kk)
        up = live & ge
        dn = live & jnp.logical_not(ge)
        return dict(
            lo=jnp.where(up, mid, lo), c_lo=jnp.where(up, cnt, st["c_lo"]),
            hi=jnp.where(dn, mid, hi), c_hi=jnp.where(dn, cnt, st["c_hi"]),
            res=jnp.where(hit, 1.0, st["res"]), thr=jnp.where(hit, mid, st["thr"]),
            need=st["need"], pivot=st["pivot"], use_pivot=jnp.zeros((1, Q), F32),
            npass=st["npass"] + 1.0)

    def round_body(carry):
        it, st = carry
        st = lax.fori_loop(0, jnp.where(it == 0, FIRST_PASSES, NEXT_PASSES), bisect_pass, st)
        a, b = bracket_ends(st["lo"], st["hi"])
        live = st["res"] == 0
        tie = live & (a == b)
        st = dict(st)
        last = live & (kk - st["c_hi"] == 1.0)
        st["thr"] = jnp.where(tie, a, jnp.where(last, b, st["thr"]))
        st["need"] = jnp.where(tie | last, kk - st["c_hi"], st["need"])
        st["res"] = jnp.where(tie | last, 1.0, st["res"])
        st["lo"] = jnp.where(live, a, st["lo"])
        st["pivot"] = b
        st["use_pivot"] = jnp.ones((1, Q), F32)
        return it + 1, st

    def round_cond(carry):
        _, st = carry
        return jnp.min(st["res"]) == 0

    _, st = lax.while_loop(round_cond, round_body, (jnp.int32(0), state0))
    thr = st["thr"]
    need0 = st["need"]

    qt = qt_ref[0, 0]
    rowhead = lax.broadcasted_iota(jnp.int32, (256, Q), 0) // HEAD_DIM
    zero = jnp.zeros((), BF16)
    for pr in range(N_HEADS // 2):
        qg = qt[(pr // 2) * 256:(pr // 2 + 1) * 256, :]
        qp_ref[pr] = jnp.concatenate([jnp.where(rowhead == (2 * pr) % 4, qg, zero),
                                      jnp.where(rowhead == (2 * pr + 1) % 4, qg, zero)], axis=1)
    acc_ref[...] = jnp.zeros(acc_ref.shape, F32)
    ri = lax.broadcasted_iota(jnp.int32, (KC, KC), 0)
    ci = lax.broadcasted_iota(jnp.int32, (KC, KC), 1)
    tril = jnp.where(ci <= ri, 1.0, 0.0).astype(BF16)

    def bias_body(u, need):
        for i in range(BIAS_UNROLL):
            r0 = pl.multiple_of((u * BIAS_UNROLL + i) * KC, KC)
            sv = sc_ref[pl.ds(r0, KC), :]
            eq = sv == thr
            tie = jnp.where(eq, 1.0, 0.0)
            prefix = _dot(tril, tie.astype(BF16))
            take_tie = jnp.where(prefix <= need, 0.0, MASK_BIAS)
            sc_ref[pl.ds(r0, KC), :] = jnp.where(sv > thr, 0.0, jnp.where(eq, take_tie, MASK_BIAS))
            need = need - jnp.sum(_fold_rows(tie, jnp.add), axis=0, keepdims=True)
        return need

    lax.fori_loop(0, nbias, bias_body, need0)

    def logits_into(c, lg_ref):
        r0 = pl.multiple_of(c * KC, KC)
        bias = sc_ref[pl.ds(r0, KC), :]
        bias2 = jnp.concatenate([bias, bias], axis=1)
        for pr in range(N_HEADS // 2):
            g0 = (pr // 2) * 256
            lg_ref[pr] = _dot(k_ref[0, pl.ds(r0, KC), g0:g0 + 256], qp_ref[pr]) + bias2

    def chunk_max(lg_ref):
        return tuple(jnp.max(_fold_rows(lg_ref[pr], jnp.maximum), axis=0, keepdims=True)
                     for pr in range(N_HEADS // 2))

    def softmax_step(c, lg_ref, ms, ls, cmax):
        new_ms, new_ls = [], []
        for pr in range(N_HEADS // 2):
            m_new = jnp.maximum(ms[pr], cmax[pr])
            alpha = jnp.exp2(ms[pr] - m_new)
            for r in range(0, KC, 2 * SUBLANES):
                pe = jnp.exp2(lg_ref[pr, r:r + 2 * SUBLANES, :] - m_new)
                p_ref[pr, r:r + 2 * SUBLANES, :] = pe.astype(BF16)
            o = _dot(vt_ref[0, c, pr * PAIR_ROWS:(pr + 1) * PAIR_ROWS, :], p_ref[pr])
            new_ls.append(alpha * ls[pr] + o[128:129, :])
            new_ms.append(m_new)
            a0 = pr * 128
            acc_ref[a0:a0 + 64, :] = acc_ref[a0:a0 + 64, :] * alpha[:, 0:Q] + o[0:64, 0:Q]
            acc_ref[a0 + 64:a0 + 128, :] = (acc_ref[a0 + 64:a0 + 128, :] * alpha[:, Q:2 * Q]
                                            + o[64:128, Q:2 * Q])
        return tuple(new_ms), tuple(new_ls)

    logits_into(0, lga_ref)

    def attn_body(i, carry):
        ms, ls, cmax = carry
        c = 2 * i
        logits_into(c + 1, lgb_ref)
        ms, ls = softmax_step(c, lga_ref, ms, ls, cmax)
        logits_into(jnp.minimum(c + 2, 2 * nscan - 1), lga_ref)
        ms, ls = softmax_step(c + 1, lgb_ref, ms, ls, chunk_max(lgb_ref))
        return ms, ls, chunk_max(lga_ref)

    ms0 = tuple(jnp.full((1, 2 * Q), MASK_BIAS, F32) for _ in range(N_HEADS // 2))
    ls0 = tuple(jnp.zeros((1, 2 * Q), F32) for _ in range(N_HEADS // 2))
    _, ls, _ = lax.fori_loop(0, nscan, attn_body, (ms0, ls0, chunk_max(lga_ref)))
    outs = []
    for pr in range(N_HEADS // 2):
        a0 = pr * 128
        outs.append(acc_ref[a0:a0 + 64, :] / ls[pr][:, 0:Q])
        outs.append(acc_ref[a0 + 64:a0 + 128, :] / ls[pr][:, Q:2 * Q])
    o_ref[0] = jnp.concatenate(outs, axis=0).T * _silu(ag_ref[0])


def _attention(qt, qi, wt, ag, ws, k, vt, topk):
    B, S, _ = k.shape
    nq = S // Q_BLOCK
    nch = S // KEY_CHUNK
    qblk = lambda r, w: pl.BlockSpec((1, 1, r, w), lambda b, j: (b, j, 0, 0))
    once = pl.Buffered(1)
    seq = lambda w: pl.BlockSpec((1, S, w), lambda b, j: (b, 0, 0), pipeline_mode=once)
    return pl.pallas_call(
        functools.partial(_attn_kernel, topk=topk),
        grid=(B, nq),
        in_specs=[
            qblk(ATTN_W, Q_BLOCK), qblk(128, IDX_HEADS * Q_BLOCK), qblk(SUBLANES, Q_BLOCK),
            pl.BlockSpec((1, Q_BLOCK, ATTN_W), lambda b, j: (b, j, 0)),
            seq(128), seq(ATTN_W),
            pl.BlockSpec((1, nch, VT_ROWS, KEY_CHUNK), lambda b, j: (b, 0, 0, 0), pipeline_mode=once),
        ],
        out_specs=pl.BlockSpec((1, Q_BLOCK, ATTN_W), lambda b, j: (b, j, 0)),
        out_shape=jax.ShapeDtypeStruct((B, S, ATTN_W), F32),
        scratch_shapes=[
            pltpu.VMEM((S, Q_BLOCK), F32),
            pltpu.VMEM((N_HEADS // 2, 256, 2 * Q_BLOCK), BF16),
            pltpu.VMEM((ATTN_W, Q_BLOCK), F32),
            pltpu.VMEM((N_HEADS // 2, KEY_CHUNK, 2 * Q_BLOCK), F32),
            pltpu.VMEM((N_HEADS // 2, KEY_CHUNK, 2 * Q_BLOCK), F32),
            pltpu.VMEM((N_HEADS // 2, KEY_CHUNK, 2 * Q_BLOCK), BF16),
        ],
        compiler_params=pltpu.CompilerParams(
            dimension_semantics=("arbitrary", "arbitrary"), vmem_limit_bytes=VMEM_LIMIT),
        name="attention",
    )(qt, qi, wt, ag, ws, k, vt)


def _outproj_kernel(h_ref, yab_ref, yc_ref, p_ref, wo_ref, plew_ref, pleg_ref, fng_ref, o_ref,
                    *, final):
    h = (h_ref[0] + _dot(yab_ref[0].astype(BF16), wo_ref[0:512, :])
         + _dot(yc_ref[0].astype(BF16), wo_ref[512:1024, :]))
    ple = _dot(p_ref[0].astype(BF16), plew_ref[...])
    h = h + ple * _sigmoid(_dot(h.astype(BF16), pleg_ref[...]))
    if final:
        h = h * lax.rsqrt(jnp.mean(h * h, axis=-1, keepdims=True) + EPS) * fng_ref[...]
    o_ref[0] = h


def _outproj(h, yab, yc, p, wo, plew, pleg, fng, final):
    B, S, D = h.shape
    nt = S // ROW_TILE
    row = lambda w: pl.BlockSpec((1, ROW_TILE, w), lambda b, t: (b, t, 0))
    full = lambda a: pl.BlockSpec(a.shape, lambda b, t: (0,) * a.ndim)
    return pl.pallas_call(
        functools.partial(_outproj_kernel, final=final),
        grid=(B, nt),
        in_specs=[row(D), row(512), row(512), row(p.shape[-1]), full(wo), full(plew), full(pleg),
                  full(fng)],
        out_specs=row(D),
        out_shape=jax.ShapeDtypeStruct((B, S, D), F32),
        compiler_params=pltpu.CompilerParams(
            dimension_semantics=("arbitrary", "arbitrary"), vmem_limit_bytes=VMEM_LIMIT),
        name="outproj",
    )(h, yab, yc, p, wo, plew, pleg, fng)


def _block_diag(w):
    g, a, b = w.shape
    out = jnp.zeros((g * a, g * b), w.dtype)
    for i in range(g):
        out = out.at[i * a:(i + 1) * a, i * b:(i + 1) * b].set(w[i])
    return out


def kernel(x, p, norm_g, w_in, b_in, conv_dw_w, conv_dw_b, conv_ln_g, conv_ln_b, conv_pw_w,
           conv_pw_b, pool_w, pool_b, pool_scale, w_out, ple_w, ple_gate_w, final_norm_g):
    B, S, D = x.shape
    depth = w_in.shape[0]
    assert S % ROW_TILE == 0 and D == 1024 and w_in.shape[2] == N_MAIN + 324
    topk = min(TOPK_MAX, S // 4)

    half = HEAD_DIM // 2
    inv = ROPE_THETA ** (-jnp.arange(half, dtype=F32) / half)
    ang = jnp.arange(S).astype(F32)[:, None] * inv[None, :]
    cos = jnp.tile(jnp.cos(ang), (1, 4))
    sin = jnp.tile(jnp.concatenate([-jnp.sin(ang), jnp.sin(ang)], axis=1), (1, 2))

    row2 = lambda a: a.reshape(1, -1)
    h = x
    for i in range(depth):
        wm = w_in[i, :, :N_MAIN].astype(BF16)
        bm = row2(b_in[i, :N_MAIN])
        wx = jnp.pad(w_in[i, :, N_MAIN:], ((0, 0), (0, IDX_PAD - 324)))
        wxh = wx.astype(BF16)
        bx = row2(jnp.pad(b_in[i, N_MAIN:], (0, IDX_PAD - 324)))
        zc, zp, qt, k, vt, ag, qi, ws, wt = _inproj(
            h, row2(norm_g[i]), wm, bm, wxh, (wx - wxh.astype(F32)).astype(BF16), bx, cos, sin)
        yab = _convpool(zc, zp, conv_dw_w[i], row2(conv_dw_b[i]), row2(conv_ln_g[i]),
                        row2(conv_ln_b[i]), conv_pw_w[i].astype(BF16), row2(conv_pw_b[i]),
                        _block_diag(pool_w[i]).astype(BF16), row2(pool_b[i]), row2(pool_scale[i]))
        yc = _attention(qt, qi, wt, ag, ws, k, vt, topk)
        h = _outproj(h, yab, yc, p[i], w_out[i].astype(BF16), ple_w[i].astype(BF16),
                     ple_gate_w[i].astype(BF16), row2(final_norm_g), final=(i == depth - 1))
    return h
```

```python
import functools

import jax
import jax.numpy as jnp
from jax import lax
from jax.experimental import pallas as pl
from jax.experimental.pallas import tpu as pltpu

F32 = jnp.float32
BF16 = jnp.bfloat16

CONV_W = 256
CONV_K = 31
POOL_W = 256
POOL_GW = 64
POOL_WINDOWS = (2, 4, 8, 16)
HEAD_DIM = 64
ATTN_W = 512
N_HEADS = 8
IDX_HEADS = 4
IDX_DIM = 64
TOPK_MAX = 256
ROPE_THETA = 10000.0
EPS = 1e-6

SUBLANES = 8
Q_BLOCK = 128
KEY_CHUNK = 256
SCAN_ROWS = 2 * KEY_CHUNK
BIAS_UNROLL = 2 * SCAN_ROWS // KEY_CHUNK
PAIR_ROWS = 128 + 16
VT_ROWS = (N_HEADS // 2) * PAIR_ROWS
FIRST_PASSES = 16
NEXT_PASSES = 2
INTERP_AFTER = 5
INTERP_CLIP = 0.25
LOG2E = 1.4426950408889634
HALO = 32
ROW_TILE = 512
IDX_PAD = 384
N_MAIN = 3 * CONV_W + 2 * POOL_W + 4 * ATTN_W
MASK_BIAS = -1e30
NEED_ALL = 1e9
VMEM_LIMIT = 60 * 1024 * 1024


def _sigmoid(x):
    return 1.0 / (1.0 + jnp.exp(-x))


def _silu(x):
    return x * _sigmoid(x)


def _dot(a, b):
    return lax.dot_general(a, b, (((1,), (0,)), ((), ())), preferred_element_type=F32)


def _rope_slab(x, cos, sin):
    lane = lax.broadcasted_iota(jnp.int32, x.shape, 1)
    swapped = jnp.where((lane & 32) == 0, pltpu.roll(x, 96, 1), pltpu.roll(x, 32, 1))
    return x * cos + swapped * sin


def _rope(x, cos, sin):
    return jnp.concatenate(
        [_rope_slab(x[:, s:s + 128], cos, sin) for s in range(0, x.shape[1], 128)], axis=1)


def _inproj_kernel(h_ref, *refs):
    _inproj_body(h_ref[0], *refs)


def _outin_kernel(h_ref, yab_ref, yc_ref, p_ref, wo_ref, plew_ref, pleg_ref, *refs):
    h = _residual_update(h_ref, yab_ref, yc_ref, p_ref, wo_ref, plew_ref, pleg_ref)
    n_in = 8
    refs[n_in][0] = h
    _inproj_body(h, *refs[:n_in], *refs[n_in + 1:])


def _residual_update(h_ref, yab_ref, yc_ref, p_ref, wo_ref, plew_ref, pleg_ref):
    h = (h_ref[0] + _dot(yab_ref[0].astype(BF16), wo_ref[0:512, :])
         + _dot(yc_ref[0].astype(BF16), wo_ref[512:1024, :]))
    ple = _dot(p_ref[0].astype(BF16), plew_ref[...])
    return h + ple * _sigmoid(_dot(h.astype(BF16), pleg_ref[...]))


def _inproj_body(x, g_ref, wm_ref, bm_ref, wxh_ref, wxl_ref, bx_ref, cos_ref, sin_ref,
                 zc_ref, zp_ref, qt_ref, k_ref, vt_ref, ag_ref, qi_ref, ws_ref, wt_ref):
    n = x * lax.rsqrt(jnp.mean(x * x, axis=-1, keepdims=True) + EPS) * g_ref[...]
    nb = n.astype(BF16)
    cos = cos_ref[...]
    sin = sin_ref[...]

    def proj(a, b):
        return _dot(nb, wm_ref[:, a:b]) + bm_ref[:, a:b]

    zc_ref[0] = proj(0, 768)
    zp_ref[0] = proj(768, 1280)
    qt = (_rope(proj(1280, 1792), cos, sin) * (HEAD_DIM ** -0.5 * LOG2E)).T.astype(BF16)
    for jb in range(ROW_TILE // Q_BLOCK):
        qt_ref[0, jb] = qt[:, jb * Q_BLOCK:(jb + 1) * Q_BLOCK]
    k_ref[0] = _rope(proj(1792, 2304), cos, sin).astype(BF16)
    vt = proj(2304, 2816).T.astype(BF16)
    ones = jnp.ones((PAIR_ROWS - 128, KEY_CHUNK), BF16)
    for c in range(ROW_TILE // KEY_CHUNK):
        vt_ref[0, c] = jnp.concatenate(
            [piece for pr in range(N_HEADS // 2)
             for piece in (vt[pr * 128:(pr + 1) * 128, c * KEY_CHUNK:(c + 1) * KEY_CHUNK], ones)], axis=0)
    ag_ref[0] = proj(2816, 3328)

    nl = (n - nb.astype(F32)).astype(BF16)
    zx = (_dot(nb, wxh_ref[...]) + _dot(nb, wxl_ref[...]) + _dot(nl, wxh_ref[...])) + bx_ref[...]
    qit = (_rope(zx[:, 0:256], cos, sin) * (IDX_DIM ** -0.5)).T.astype(BF16)
    pad = jnp.zeros((128 - IDX_DIM, Q_BLOCK), BF16)
    for jb in range(ROW_TILE // Q_BLOCK):
        qi_ref[0, jb] = jnp.concatenate(
            [jnp.concatenate([qit[h * IDX_DIM:(h + 1) * IDX_DIM, jb * Q_BLOCK:(jb + 1) * Q_BLOCK],
                              pad], axis=0) for h in range(IDX_HEADS)], axis=1)
    tail = zx[:, 256:384]
    lane = lax.broadcasted_iota(jnp.int32, tail.shape, 1)
    tail = jnp.where(lane < IDX_DIM, _rope_slab(tail, cos, sin), tail)
    ws_ref[0] = tail.astype(BF16)
    wt = tail.T[IDX_DIM:IDX_DIM + SUBLANES, :]
    for jb in range(ROW_TILE // Q_BLOCK):
        wt_ref[0, jb] = wt[:, jb * Q_BLOCK:(jb + 1) * Q_BLOCK]


def _inproj(h, g, wm, bm, wxh, wxl, bx, cos, sin, prev=None):
    B, S, D = h.shape
    nt = S // ROW_TILE
    cpt = ROW_TILE // KEY_CHUNK
    qpt = ROW_TILE // Q_BLOCK
    nch = S // KEY_CHUNK
    nq = S // Q_BLOCK
    row = lambda w: pl.BlockSpec((1, ROW_TILE, w), lambda b, t: (b, t, 0))
    full = lambda a: pl.BlockSpec(a.shape, lambda b, t: (0,) * a.ndim)
    tab = pl.BlockSpec((ROW_TILE, 128), lambda b, t: (t, 0))
    qblk = lambda r, w: pl.BlockSpec((1, qpt, r, w), lambda b, t: (b, t, 0, 0))
    out_shape = (
        jax.ShapeDtypeStruct((B, S, 768), F32),
        jax.ShapeDtypeStruct((B, S, 512), F32),
        jax.ShapeDtypeStruct((B, nq, ATTN_W, Q_BLOCK), BF16),
        jax.ShapeDtypeStruct((B, S, ATTN_W), BF16),
        jax.ShapeDtypeStruct((B, nch, VT_ROWS, KEY_CHUNK), BF16),
        jax.ShapeDtypeStruct((B, S, ATTN_W), F32),
        jax.ShapeDtypeStruct((B, nq, 128, IDX_HEADS * Q_BLOCK), BF16),
        jax.ShapeDtypeStruct((B, S, 128), BF16),
        jax.ShapeDtypeStruct((B, nq, SUBLANES, Q_BLOCK), F32),
    )
    out_specs = (
        row(768), row(512), qblk(ATTN_W, Q_BLOCK), row(ATTN_W),
        pl.BlockSpec((1, cpt, VT_ROWS, KEY_CHUNK), lambda b, t: (b, t, 0, 0)),
        row(ATTN_W), qblk(128, IDX_HEADS * Q_BLOCK), row(128), qblk(SUBLANES, Q_BLOCK),
    )
    in_specs = [full(g), full(wm), full(bm), full(wxh), full(wxl), full(bx), tab, tab]
    operands = (g, wm, bm, wxh, wxl, bx, cos, sin)
    if prev is None:
        body, name = _inproj_kernel, "inproj"
        in_specs = [row(D)] + in_specs
        operands = (h,) + operands
    else:
        yab, yc, p, wo, plew, pleg = prev
        body, name = _outin_kernel, "outproj_inproj"
        in_specs = [row(D), row(512), row(512), row(p.shape[-1]), full(wo), full(plew),
                    full(pleg)] + in_specs
        operands = (h, yab, yc, p, wo, plew, pleg) + operands
        out_specs = (row(D),) + out_specs
        out_shape = (jax.ShapeDtypeStruct((B, S, D), F32),) + out_shape
    return pl.pallas_call(
        body,
        grid=(B, nt),
        in_specs=in_specs,
        out_specs=out_specs,
        out_shape=out_shape,
        compiler_params=pltpu.CompilerParams(
            dimension_semantics=("arbitrary", "arbitrary"), vmem_limit_bytes=VMEM_LIMIT),
        name=name,
    )(*operands)


def _convpool_kernel(zc_ref, zch_ref, zp_ref, zph_ref, dw_ref, dwb_ref, lng_ref, lnb_ref,
                     pw_ref, pwb_ref, plw_ref, plb_ref, pls_ref, o_ref, ubuf, xbuf, shift):
    t = pl.program_id(1)
    keep = (t > 0).astype(F32)

    zc = zc_ref[0]
    zch = zch_ref[0]
    ubuf[0:HALO] = zch[:, 0:256] * _sigmoid(zch[:, 256:512]) * keep
    ubuf[HALO:HALO + ROW_TILE] = zc[:, 0:256] * _sigmoid(zc[:, 256:512])
    def window(buf, off):
        r = off % SUBLANES
        return (buf[off:off + ROW_TILE, :] if r == 0
                else shift[r - 1, off - r:off - r + ROW_TILE, :])

    def make_shifts(buf):
        for r in range(1, SUBLANES):
            shift[r - 1] = buf[pl.ds(r, HALO + ROW_TILE - SUBLANES), :]

    make_shifts(ubuf)
    acc = jnp.zeros((ROW_TILE, CONV_W), F32) + dwb_ref[...]
    for j in range(CONV_K):
        acc = acc + window(ubuf, HALO - (CONV_K - 1) + j) * dw_ref[j:j + 1, :]
    mu = jnp.mean(acc, axis=-1, keepdims=True)
    xc = acc - mu
    y = xc * lax.rsqrt(jnp.mean(xc * xc, axis=-1, keepdims=True) + EPS) * lng_ref[...] + lnb_ref[...]
    y = _silu(y)
    ya = (_dot(y.astype(BF16), pw_ref[...]) + pwb_ref[...]) * _silu(zc[:, 512:768])

    zp = zp_ref[0]
    x = zp[:, 0:256]
    xbuf[0:HALO] = zph_ref[0][:, 0:256] * keep
    xbuf[HALO:HALO + ROW_TILE] = x
    make_shifts(xbuf)
    run = x
    sums = {}
    for d in range(1, POOL_WINDOWS[-1]):
        run = run + window(xbuf, HALO - d)
        if d + 1 in POOL_WINDOWS:
            sums[d + 1] = run
    lane = lax.broadcasted_iota(jnp.int32, (ROW_TILE, POOL_W), 1)
    grp = lane // POOL_GW
    wsum = jnp.where(grp == 0, sums[2], jnp.where(grp == 1, sums[4],
                                                  jnp.where(grp == 2, sums[8], sums[16])))
    win = jnp.where(grp == 0, 2, jnp.where(grp == 1, 4, jnp.where(grp == 2, 8, 16)))
    pos = t * ROW_TILE + lax.broadcasted_iota(jnp.int32, (ROW_TILE, POOL_W), 0)
    cnt = jnp.minimum(pos + 1, win).astype(F32)
    d = wsum / cnt - x
    yb = (_dot(d.astype(BF16), plw_ref[...]) + plb_ref[...]) * pls_ref[...] * _silu(zp[:, 256:512])

    o_ref[0] = jnp.concatenate([ya, yb], axis=1)


def _convpool(zc, zp, dw, dwb, lng, lnb, pw, pwb, plw, plb, pls):
    B, S, _ = zc.shape
    nt = S // ROW_TILE
    hpt = ROW_TILE // HALO
    row = lambda w: pl.BlockSpec((1, ROW_TILE, w), lambda b, t: (b, t, 0))
    halo = lambda w: pl.BlockSpec((1, HALO, w), lambda b, t: (b, jnp.maximum(t * hpt - 1, 0), 0))
    full = lambda a: pl.BlockSpec(a.shape, lambda b, t: (0,) * a.ndim)
    return pl.pallas_call(
        _convpool_kernel,
        grid=(B, nt),
        in_specs=[row(768), halo(768), row(512), halo(512), full(dw), full(dwb), full(lng),
                  full(lnb), full(pw), full(pwb), full(plw), full(plb), full(pls)],
        out_specs=row(512),
        out_shape=jax.ShapeDtypeStruct((B, S, 512), F32),
        scratch_shapes=[pltpu.VMEM((HALO + ROW_TILE, CONV_W), F32),
                        pltpu.VMEM((HALO + ROW_TILE, POOL_W), F32),
                        pltpu.VMEM((SUBLANES - 1, HALO + ROW_TILE - SUBLANES, CONV_W), F32)],
        compiler_params=pltpu.CompilerParams(
            dimension_semantics=("arbitrary", "arbitrary"), vmem_limit_bytes=VMEM_LIMIT),
        name="convpool",
    )(zc, zc, zp, zp, dw, dwb, lng, lnb, pw, pwb, plw, plb, pls)


def _fold_rows(x, op, fn=lambda p: p):
    return _combine([fn(x[r:r + SUBLANES, :]) for r in range(0, x.shape[0], SUBLANES)], op)


def _combine(parts, op):
    while len(parts) > 1:
        parts = [op(parts[i], parts[i + 1]) for i in range(0, len(parts) - 1, 2)] + (
            [parts[-1]] if len(parts) % 2 else [])
    return parts[0]


def _attn_kernel(qt_ref, qi_ref, wt_ref, ag_ref, ws_ref, k_ref, vt_ref, o_ref,
                 sc_ref, qp_ref, acc_ref, lga_ref, lgb_ref, p_ref, *, topk):
    j = pl.program_id(1)
    nfull = j // 2
    nch = nfull + 1
    Q = Q_BLOCK
    KC = KEY_CHUNK
    qpos = j * Q + lax.broadcasted_iota(jnp.int32, (1, Q), 1)

    qi = qi_ref[0, 0]
    wt = wt_ref[0, 0]
    w_h = [wt[h:h + 1, :] for h in range(IDX_HEADS)]

    def unit_scores(u):
        r0 = pl.multiple_of(u * SCAN_ROWS, SCAN_ROWS)
        s = _dot(ws_ref[0, pl.ds(r0, SCAN_ROWS), :], qi)
        tot = jnp.zeros((SCAN_ROWS, Q), F32)
        for h in range(IDX_HEADS):
            tot = tot + w_h[h] * jnp.maximum(s[:, h * Q:(h + 1) * Q], 0.0)
        return tot

    def score_body(i, carry):
        mx, mn = carry
        for u in (2 * i, 2 * i + 1):
            s = unit_scores(u)
            sc_ref[pl.ds(pl.multiple_of(u * SCAN_ROWS, SCAN_ROWS), SCAN_ROWS), :] = s
            mx = jnp.maximum(mx, _fold_rows(s, jnp.maximum))
            mn = jnp.minimum(mn, _fold_rows(s, jnp.minimum))
        return mx, mn

    nscan = (nch + 1) // 2
    nbias = (nch + BIAS_UNROLL - 1) // BIAS_UNROLL
    init = (jnp.full((SUBLANES, Q), -jnp.inf, F32), jnp.full((SUBLANES, Q), jnp.inf, F32))
    mx, mn = lax.fori_loop(0, nbias, score_body, init)
    rowmax = jnp.max(mx, axis=0, keepdims=True)
    rowmin = jnp.min(mn, axis=0, keepdims=True)

    d0 = pl.multiple_of(nfull * KC, KC)
    kpos = nfull * KC + lax.broadcasted_iota(jnp.int32, (KC, Q), 0)
    sc_ref[pl.ds(d0, KC), :] = jnp.where(kpos <= qpos, sc_ref[pl.ds(d0, KC), :], -jnp.inf)

    def pad_body(c, _):
        sc_ref[pl.ds(pl.multiple_of(c * KC, KC), KC), :] = jnp.full((KC, Q), -jnp.inf, F32)
        return 0

    lax.fori_loop(nch, nbias * BIAS_UNROLL, pad_body, 0)

    ncausal = (qpos + 1).astype(F32)
    kk = jnp.minimum(ncausal, float(topk))

    def count_ge(t):
        tb = jnp.broadcast_to(t, (SUBLANES, Q))

        def body(c, acc):
            for u in (2 * c, 2 * c + 1):
                sv = sc_ref[pl.ds(pl.multiple_of(u * SCAN_ROWS, SCAN_ROWS), SCAN_ROWS), :]
                acc = acc + _fold_rows(sv, jnp.add, lambda p: jnp.where(p >= tb, 1.0, 0.0))
            return acc

        acc = lax.fori_loop(0, nbias, body, jnp.zeros((SUBLANES, Q), F32))
        return jnp.sum(acc, axis=0, keepdims=True)

    def bracket_ends(lo, hi):
        lob = jnp.broadcast_to(lo, (SUBLANES, Q))
        hib = jnp.broadcast_to(hi, (SUBLANES, Q))

        def body(c, carry):
            a, b = carry
            sv = sc_ref[pl.ds(pl.multiple_of(c * SCAN_ROWS, SCAN_ROWS), SCAN_ROWS), :]
            a = jnp.minimum(a, _fold_rows(sv, jnp.minimum, lambda p: jnp.where(p >= lob, p, jnp.inf)))
            b = jnp.maximum(b, _fold_rows(sv, jnp.maximum, lambda p: jnp.where(p < hib, p, -jnp.inf)))
            return a, b

        a, b = lax.fori_loop(0, nscan, body, (jnp.full((SUBLANES, Q), jnp.inf, F32),
                                            jnp.full((SUBLANES, Q), -jnp.inf, F32)))
        return jnp.min(a, axis=0, keepdims=True), jnp.max(b, axis=0, keepdims=True)

    res0 = (ncausal <= kk).astype(F32)
    state0 = dict(lo=rowmin, hi=jnp.full((1, Q), jnp.inf, F32), c_lo=ncausal,
                  c_hi=jnp.zeros((1, Q), F32), res=res0, thr=rowmin,
                  need=jnp.full((1, Q), NEED_ALL, F32), pivot=rowmax,
                  use_pivot=jnp.ones((1, Q), F32), npass=jnp.zeros((1, Q), F32))

    def bisect_pass(_, st):
        lo, hi = st["lo"], st["hi"]
        bounded = hi < jnp.inf
        frac = jnp.clip((st["c_lo"] - kk + 0.5) / jnp.maximum(st["c_lo"] - st["c_hi"], 1.0),
                        INTERP_CLIP, 1.0 - INTERP_CLIP)
        frac = jnp.where(st["npass"] >= INTERP_AFTER, frac, 0.5)
        mid = jnp.where(st["use_pivot"] > 0, st["pivot"],
                        jnp.where(bounded, lo + (hi - lo) * frac, lo))
        cnt = count_ge(mid)
        live = st["res"] == 0
        ge = cnt >= kk
        hit = live & (cnt == kk)
        up = live & ge
        dn = live & jnp.logical_not(ge)
        return dict(
            lo=jnp.where(up, mid, lo), c_lo=jnp.where(up, cnt, st["c_lo"]),
            hi=jnp.where(dn, mid, hi), c_hi=jnp.where(dn, cnt, st["c_hi"]),
            res=jnp.where(hit, 1.0, st["res"]), thr=jnp.where(hit, mid, st["thr"]),
            need=st["need"], pivot=st["pivot"], use_pivot=jnp.zeros((1, Q), F32),
            npass=st["npass"] + 1.0)

    def round_body(carry):
        it, st = carry
        st = lax.fori_loop(0, jnp.where(it == 0, FIRST_PASSES, NEXT_PASSES), bisect_pass, st)
        a, b = bracket_ends(st["lo"], st["hi"])
        live = st["res"] == 0
        tie = live & (a == b)
        st = dict(st)
        last = live & (kk - st["c_hi"] == 1.0)
        st["thr"] = jnp.where(tie, a, jnp.where(last, b, st["thr"]))
        st["need"] = jnp.where(tie | last, kk - st["c_hi"], st["need"])
        st["res"] = jnp.where(tie | last, 1.0, st["res"])
        st["lo"] = jnp.where(live, a, st["lo"])
        st["pivot"] = b
        st["use_pivot"] = jnp.ones((1, Q), F32)
        return it + 1, st

    def round_cond(carry):
        _, st = carry
        return jnp.min(st["res"]) == 0

    _, st = lax.while_loop(round_cond, round_body, (jnp.int32(0), state0))
    thr = st["thr"]
    need0 = st["need"]

    qt = qt_ref[0, 0]
    rowhead = lax.broadcasted_iota(jnp.int32, (256, Q), 0) // HEAD_DIM
    zero = jnp.zeros((), BF16)
    for pr in range(N_HEADS // 2):
        qg = qt[(pr // 2) * 256:(pr // 2 + 1) * 256, :]
        qp_ref[pr] = jnp.concatenate([jnp.where(rowhead == (2 * pr) % 4, qg, zero),
                                      jnp.where(rowhead == (2 * pr + 1) % 4, qg, zero)], axis=1)
    acc_ref[...] = jnp.zeros(acc_ref.shape, F32)
    ri = lax.broadcasted_iota(jnp.int32, (KC, KC), 0)
    ci = lax.broadcasted_iota(jnp.int32, (KC, KC), 1)
    tril = jnp.where(ci <= ri, 1.0, 0.0).astype(BF16)

    def bias_body(u, need):
        for i in range(BIAS_UNROLL):
            r0 = pl.multiple_of((u * BIAS_UNROLL + i) * KC, KC)
            sv = sc_ref[pl.ds(r0, KC), :]
            eq = sv == thr
            tie = jnp.where(eq, 1.0, 0.0)
            prefix = _dot(tril, tie.astype(BF16))
            take_tie = jnp.where(prefix <= need, 0.0, MASK_BIAS)
            sc_ref[pl.ds(r0, KC), :] = jnp.where(sv > thr, 0.0, jnp.where(eq, take_tie, MASK_BIAS))
            need = need - jnp.sum(_fold_rows(tie, jnp.add), axis=0, keepdims=True)
        return need

    lax.fori_loop(0, nbias, bias_body, need0)

    def logits_into(c, lg_ref):
        r0 = pl.multiple_of(c * KC, KC)
        bias = sc_ref[pl.ds(r0, KC), :]
        bias2 = jnp.concatenate([bias, bias], axis=1)
        for pr in range(N_HEADS // 2):
            g0 = (pr // 2) * 256
            lg_ref[pr] = _dot(k_ref[0, pl.ds(r0, KC), g0:g0 + 256], qp_ref[pr]) + bias2

    def chunk_max(lg_ref):
        return tuple(jnp.max(_fold_rows(lg_ref[pr], jnp.maximum), axis=0, keepdims=True)
                     for pr in range(N_HEADS // 2))

    def softmax_step(c, lg_ref, ms, ls, cmax):
        new_ms, new_ls = [], []
        for pr in range(N_HEADS // 2):
            m_new = jnp.maximum(ms[pr], cmax[pr])
            alpha = jnp.exp2(ms[pr] - m_new)
            for r in range(0, KC, 2 * SUBLANES):
                pe = jnp.exp2(lg_ref[pr, r:r + 2 * SUBLANES, :] - m_new)
                p_ref[pr, r:r + 2 * SUBLANES, :] = pe.astype(BF16)
            o = _dot(vt_ref[0, c, pr * PAIR_ROWS:(pr + 1) * PAIR_ROWS, :], p_ref[pr])
            new_ls.append(alpha * ls[pr] + o[128:129, :])
            new_ms.append(m_new)
            a0 = pr * 128
            acc_ref[a0:a0 + 64, :] = acc_ref[a0:a0 + 64, :] * alpha[:, 0:Q] + o[0:64, 0:Q]
            acc_ref[a0 + 64:a0 + 128, :] = (acc_ref[a0 + 64:a0 + 128, :] * alpha[:, Q:2 * Q]
                                            + o[64:128, Q:2 * Q])
        return tuple(new_ms), tuple(new_ls)

    logits_into(0, lga_ref)

    def attn_body(i, carry):
        ms, ls, cmax = carry
        c = 2 * i
        logits_into(c + 1, lgb_ref)
        ms, ls = softmax_step(c, lga_ref, ms, ls, cmax)
        logits_into(jnp.minimum(c + 2, 2 * nscan - 1), lga_ref)
        ms, ls = softmax_step(c + 1, lgb_ref, ms, ls, chunk_max(lgb_ref))
        return ms, ls, chunk_max(lga_ref)

    ms0 = tuple(jnp.full((1, 2 * Q), MASK_BIAS, F32) for _ in range(N_HEADS // 2))
    ls0 = tuple(jnp.zeros((1, 2 * Q), F32) for _ in range(N_HEADS // 2))
    _, ls, _ = lax.fori_loop(0, nscan, attn_body, (ms0, ls0, chunk_max(lga_ref)))
    outs = []
    for pr in range(N_HEADS // 2):
        a0 = pr * 128
        outs.append(acc_ref[a0:a0 + 64, :] / ls[pr][:, 0:Q])
        outs.append(acc_ref[a0 + 64:a0 + 128, :] / ls[pr][:, Q:2 * Q])
    o_ref[0] = jnp.concatenate(outs, axis=0).T * _silu(ag_ref[0])


def _attention(qt, qi, wt, ag, ws, k, vt, topk):
    B, S, _ = k.shape
    nq = S // Q_BLOCK
    nch = S // KEY_CHUNK
    qblk = lambda r, w: pl.BlockSpec((1, 1, r, w), lambda b, j: (b, j, 0, 0))
    once = pl.Buffered(1)
    seq = lambda w: pl.BlockSpec((1, S, w), lambda b, j: (b, 0, 0), pipeline_mode=once)
    return pl.pallas_call(
        functools.partial(_attn_kernel, topk=topk),
        grid=(B, nq),
        in_specs=[
            qblk(ATTN_W, Q_BLOCK), qblk(128, IDX_HEADS * Q_BLOCK), qblk(SUBLANES, Q_BLOCK),
            pl.BlockSpec((1, Q_BLOCK, ATTN_W), lambda b, j: (b, j, 0)),
            seq(128), seq(ATTN_W),
            pl.BlockSpec((1, nch, VT_ROWS, KEY_CHUNK), lambda b, j: (b, 0, 0, 0), pipeline_mode=once),
        ],
        out_specs=pl.BlockSpec((1, Q_BLOCK, ATTN_W), lambda b, j: (b, j, 0)),
        out_shape=jax.ShapeDtypeStruct((B, S, ATTN_W), F32),
        scratch_shapes=[
            pltpu.VMEM((S, Q_BLOCK), F32),
            pltpu.VMEM((N_HEADS // 2, 256, 2 * Q_BLOCK), BF16),
            pltpu.VMEM((ATTN_W, Q_BLOCK), F32),
            pltpu.VMEM((N_HEADS // 2, KEY_CHUNK, 2 * Q_BLOCK), F32),
            pltpu.VMEM((N_HEADS // 2, KEY_CHUNK, 2 * Q_BLOCK), F32),
            pltpu.VMEM((N_HEADS // 2, KEY_CHUNK, 2 * Q_BLOCK), BF16),
        ],
        compiler_params=pltpu.CompilerParams(
            dimension_semantics=("arbitrary", "arbitrary"), vmem_limit_bytes=VMEM_LIMIT),
        name="attention",
    )(qt, qi, wt, ag, ws, k, vt)


def _outproj_kernel(h_ref, yab_ref, yc_ref, p_ref, wo_ref, plew_ref, pleg_ref, fng_ref, o_ref,
                    *, final):
    h = _residual_update(h_ref, yab_ref, yc_ref, p_ref, wo_ref, plew_ref, pleg_ref)
    if final:
        h = h * lax.rsqrt(jnp.mean(h * h, axis=-1, keepdims=True) + EPS) * fng_ref[...]
    o_ref[0] = h


def _outproj(h, yab, yc, p, wo, plew, pleg, fng, final):
    B, S, D = h.shape
    nt = S // ROW_TILE
    row = lambda w: pl.BlockSpec((1, ROW_TILE, w), lambda b, t: (b, t, 0))
    full = lambda a: pl.BlockSpec(a.shape, lambda b, t: (0,) * a.ndim)
    return pl.pallas_call(
        functools.partial(_outproj_kernel, final=final),
        grid=(B, nt),
        in_specs=[row(D), row(512), row(512), row(p.shape[-1]), full(wo), full(plew), full(pleg),
                  full(fng)],
        out_specs=row(D),
        out_shape=jax.ShapeDtypeStruct((B, S, D), F32),
        compiler_params=pltpu.CompilerParams(
            dimension_semantics=("arbitrary", "arbitrary"), vmem_limit_bytes=VMEM_LIMIT),
        name="outproj",
    )(h, yab, yc, p, wo, plew, pleg, fng)


def _block_diag(w):
    g, a, b = w.shape
    out = jnp.zeros((g * a, g * b), w.dtype)
    for i in range(g):
        out = out.at[i * a:(i + 1) * a, i * b:(i + 1) * b].set(w[i])
    return out


def kernel(x, p, norm_g, w_in, b_in, conv_dw_w, conv_dw_b, conv_ln_g, conv_ln_b, conv_pw_w,
           conv_pw_b, pool_w, pool_b, pool_scale, w_out, ple_w, ple_gate_w, final_norm_g):
    B, S, D = x.shape
    depth = w_in.shape[0]
    assert S % ROW_TILE == 0 and D == 1024 and w_in.shape[2] == N_MAIN + 324
    topk = min(TOPK_MAX, S // 4)

    half = HEAD_DIM // 2
    inv = ROPE_THETA ** (-jnp.arange(half, dtype=F32) / half)
    ang = jnp.arange(S).astype(F32)[:, None] * inv[None, :]
    cos = jnp.tile(jnp.cos(ang), (1, 4))
    sin = jnp.tile(jnp.concatenate([-jnp.sin(ang), jnp.sin(ang)], axis=1), (1, 2))

    row2 = lambda a: a.reshape(1, -1)
    h = x
    prev = None
    for i in range(depth):
        wm = w_in[i, :, :N_MAIN].astype(BF16)
        bm = row2(b_in[i, :N_MAIN])
        wx = jnp.pad(w_in[i, :, N_MAIN:], ((0, 0), (0, IDX_PAD - 324)))
        wxh = wx.astype(BF16)
        bx = row2(jnp.pad(b_in[i, N_MAIN:], (0, IDX_PAD - 324)))
        outs = _inproj(h, row2(norm_g[i]), wm, bm, wxh, (wx - wxh.astype(F32)).astype(BF16), bx,
                       cos, sin, prev=prev)
        if prev is not None:
            h, outs = outs[0], outs[1:]
        zc, zp, qt, k, vt, ag, qi, ws, wt = outs
        yab = _convpool(zc, zp, conv_dw_w[i], row2(conv_dw_b[i]), row2(conv_ln_g[i]),
                        row2(conv_ln_b[i]), conv_pw_w[i].astype(BF16), row2(conv_pw_b[i]),
                        _block_diag(pool_w[i]).astype(BF16), row2(pool_b[i]), row2(pool_scale[i]))
        yc = _attention(qt, qi, wt, ag, ws, k, vt, topk)
        prev = (yab, yc, p[i], w_out[i].astype(BF16), ple_w[i].astype(BF16),
                ple_gate_w[i].astype(BF16))
    return _outproj(h, *prev, row2(final_norm_g), final=True)
```
